```python
import math
import jax, jax.numpy as jnp
from jax import lax
import numpy as np

D_MODEL = 2048
BATCH = 4
SEQ = 2048
DEPTH = 2

A_HEADS = 8
A_EXPAND = 128
A_HEAD_V = 128
A_WIDTH = A_HEADS * A_EXPAND
A_CHUNK = 16
B_GROUPS = ((128, 1), (512, 4), (2048, 16))
B_HEADS_PER_GROUP = 4
B_HEADS = B_HEADS_PER_GROUP * len(B_GROUPS)
B_HEAD_DIM = 64
B_WIDTH = B_HEADS * B_HEAD_DIM
B_OUT = B_HEADS_PER_GROUP * B_HEAD_DIM
C_WIDTH = 768
C_KERNEL = 31
REL_BUCKETS = 32
REL_MAX_DIST = 2048
D_FF = 4 * D_MODEL
N_BRANCH = 3
IN_SPLITS = (A_WIDTH, A_WIDTH, A_HEADS * A_HEAD_V, A_HEADS * A_HEAD_V,
             B_WIDTH, B_WIDTH, B_WIDTH, C_WIDTH, C_WIDTH)
IN_WIDTH = sum(IN_SPLITS)
EPS = 1e-6
MASK_VALUE = -1e30
TINY = 1e-30

kernel_name = "hybrid_hgrn2_dilated_conformer_block"


def rms_norm(x, w):
    xf = x.astype(jnp.float32)
    y = xf * lax.rsqrt(jnp.mean(xf * xf, axis=-1, keepdims=True) + EPS)
    return (y * w).astype(x.dtype)


def layer_norm(x, g, b):
    xf = x.astype(jnp.float32)
    mu = jnp.mean(xf, axis=-1, keepdims=True)
    var = jnp.mean(jnp.square(xf - mu), axis=-1, keepdims=True)
    return ((xf - mu) * lax.rsqrt(var + EPS) * g + b).astype(x.dtype)


def split_cols(a, sizes):
    idx = [int(v) for v in np.cumsum(sizes)[:-1]]
    return jnp.split(a, idx, axis=-1)


def hgrn2_mixer(q_pre, f_pre, inp, og, lb, norm_w):
    f32 = jnp.float32
    Bsz, S, _ = q_pre.shape
    H, K, V, C = A_HEADS, A_EXPAND, A_HEAD_V, A_CHUNK
    N = S // C

    def chunks(a, d):
        return a.astype(f32).reshape(Bsz, N, C, H, d).transpose(0, 3, 1, 2, 4)

    lb = jnp.maximum(lb.astype(f32), 0.0).reshape(H, K)[None, :, None, None, :]
    z = chunks(f_pre, K)
    log_f = jnp.logaddexp(jnp.log(lb + TINY), jnp.log1p(-lb) + jax.nn.log_sigmoid(z))
    k = (1.0 - lb) * jax.nn.sigmoid(-z)
    q = jax.nn.silu(chunks(q_pre, K))
    v = chunks(inp, V)
    b = jnp.cumsum(log_f, axis=3)
    causal = jnp.tril(jnp.ones((C, C), bool))
    decay = jnp.exp(jnp.where(causal[:, :, None],
                              b[..., :, None, :] - b[..., None, :, :], MASK_VALUE))
    attn = jnp.einsum('bhntk,bhntsk,bhnsk->bhnts', q, decay, k)
    o = jnp.einsum('bhnts,bhnsv->bhntv', attn, v)
    b_last = b[..., -1, :]
    delta = jnp.einsum('bhnck,bhncv->bhnkv', k * jnp.exp(b_last[..., None, :] - b), v)

    def step(state, xs):
        dec, dlt = xs
        return dec[..., None] * state + dlt, state

    _, s_prev = lax.scan(step, jnp.zeros((Bsz, H, K, V), f32),
                         (jnp.moveaxis(jnp.exp(b_last), 2, 0), jnp.moveaxis(delta, 2, 0)))
    s_prev = jnp.moveaxis(s_prev, 0, 2)
    o = o + jnp.einsum('bhnck,bhnkv->bhncv', q * jnp.exp(b), s_prev)
    o = o.transpose(0, 2, 3, 1, 4).reshape(Bsz, S, H, V)
    o = rms_norm(o, norm_w) * jax.nn.silu(og.astype(f32).reshape(Bsz, S, H, V))
    return o.reshape(Bsz, S, H * V).astype(q_pre.dtype)


def t5_bucket(dist):
    exact = REL_BUCKETS // 2
    d = jnp.maximum(dist, 1).astype(jnp.float32)
    large = exact + (jnp.log(d / exact) / math.log(REL_MAX_DIST / exact)
                     * (REL_BUCKETS - exact)).astype(jnp.int32)
    return jnp.where(dist < exact, dist, jnp.clip(large, exact, REL_BUCKETS - 1))


def dilated_window_group(q, k, v, bias_tab, window, dil):
    f32 = jnp.float32
    Bsz, S, H, Dh = q.shape
    J = window // dil
    Q = J
    L = S // dil
    nb = -(-L // Q)
    Lp = nb * Q

    def strided(a):
        a = a.reshape(Bsz, L, dil, H, Dh).transpose(0, 2, 1, 3, 4)
        return jnp.pad(a, ((0, 0), (0, 0), (0, Lp - L), (0, 0), (0, 0)))

    def band(a):
        a = jnp.pad(a, ((0, 0), (0, 0), (Q, 0), (0, 0), (0, 0))).reshape(Bsz, dil, nb + 1, Q, H, Dh)
        return jnp.concatenate([a[:, :, :-1], a[:, :, 1:]], axis=3)

    qb = strided(q).reshape(Bsz, dil, nb, Q, H, Dh)
    kb, vb = band(strided(k)), band(strided(v))
    steps = jnp.arange(Q)[:, None] + Q - jnp.arange(2 * Q)[None, :]
    key_pos = jnp.arange(nb)[:, None] * Q + jnp.arange(2 * Q)[None, :] - Q
    valid = ((steps >= 0) & (steps <= J))[None] & (key_pos >= 0)[:, None, :]
    bias = bias_tab[t5_bucket(jnp.clip(steps, 0, J) * dil)].transpose(2, 0, 1)
    logits = (jnp.einsum('brnqhd,brnkhd->brnhqk', qb, kb).astype(f32) * (Dh ** -0.5)
              + bias.astype(f32))
    logits = jnp.where(valid[None, None, :, None], logits, MASK_VALUE)
    m = jnp.max(logits, axis=-1, keepdims=True)
    p = jnp.exp(logits - m)
    z = jnp.sum(p, axis=-1, keepdims=True)
    o = jnp.einsum('brnhqk,brnkhd->brnqhd', p / z, vb.astype(f32))
    lse = (m + jnp.log(z))[..., 0]
    o = o.reshape(Bsz, dil, Lp, H, Dh)[:, :, :L].transpose(0, 2, 1, 3, 4).reshape(Bsz, S, H, Dh)
    lse = (lse.transpose(0, 1, 2, 4, 3).reshape(Bsz, dil, Lp, H)[:, :, :L]
           .transpose(0, 2, 1, 3).reshape(Bsz, S, H))
    return o, lse


def dilated_attention_mixer(q, k, v, rel_bias):
    Bsz, S, _ = q.shape
    shp = (Bsz, S, B_HEADS, B_HEAD_DIM)
    q, k, v = q.reshape(shp), k.reshape(shp), v.reshape(shp)
    outs, lses = [], []
    for g, (window, dil) in enumerate(B_GROUPS):
        hs = slice(g * B_HEADS_PER_GROUP, (g + 1) * B_HEADS_PER_GROUP)
        o, lse = dilated_window_group(q[:, :, hs], k[:, :, hs], v[:, :, hs],
                                      rel_bias[:, hs], window, dil)
        outs.append(o)
        lses.append(lse)
    alpha = jax.nn.softmax(jnp.stack(lses, axis=0), axis=0)
    out = jnp.sum(alpha[..., None] * jnp.stack(outs, axis=0), axis=0)
    return out.reshape(Bsz, S, B_OUT).astype(q.dtype)


def conv_module(a, gate, conv_w, conv_b, ln_g, ln_b):
    u = a * jax.nn.sigmoid(gate)
    y = lax.conv_general_dilated(u, conv_w[:, None, :], window_strides=(1,),
                                 padding=((C_KERNEL - 1, 0),),
                                 dimension_numbers=('NWC', 'WIO', 'NWC'),
                                 feature_group_count=C_WIDTH) + conv_b
    return jax.nn.silu(layer_norm(y, ln_g, ln_b))


def setup_inputs(seed: int = 0) -> dict:
    key = jax.random.key(seed)
    ks = jax.random.split(key, 26)
    f32 = jnp.float32

    def nrm(k, shape, scale):
        return jax.random.normal(k, shape, f32) * scale

    D = D_MODEL
    return {
        "x": nrm(ks[0], (BATCH, SEQ, D), 1.0),
        "c": nrm(ks[1], (BATCH, D), 1.0),
        "rel_bias": nrm(ks[2], (REL_BUCKETS, B_HEADS), 0.5),
        "hgrn_lb_logits": nrm(ks[3], (DEPTH, A_WIDTH), 1.0),
        "w_ada": nrm(ks[4], (DEPTH, D, 6 * D), 0.3 * D ** -0.5),
        "b_ada": nrm(ks[5], (DEPTH, 6 * D), 0.02),
        "mix_norm_pre": 1.0 + nrm(ks[6], (DEPTH, D), 0.05),
        "mix_norm_post": 1.0 + nrm(ks[7], (DEPTH, D), 0.05),
        "w_in": nrm(ks[8], (DEPTH, D, IN_WIDTH), D ** -0.5),
        "w_gate": nrm(ks[9], (DEPTH, D, N_BRANCH * D), D ** -0.5),
        "b_gate": nrm(ks[10], (DEPTH, N_BRANCH * D), 0.02),
        "hgrn_norm_w": 1.0 + nrm(ks[11], (DEPTH, A_HEAD_V), 0.05),
        "conv_w": nrm(ks[12], (DEPTH, C_KERNEL, C_WIDTH), C_KERNEL ** -0.5),
        "conv_b": nrm(ks[13], (DEPTH, C_WIDTH), 0.02),
        "conv_ln_g": 1.0 + nrm(ks[14], (DEPTH, C_WIDTH), 0.05),
        "conv_ln_b": nrm(ks[15], (DEPTH, C_WIDTH), 0.02),
        "w_a_out": nrm(ks[16], (DEPTH, A_HEADS * A_HEAD_V, D), (A_HEADS * A_HEAD_V) ** -0.5),
        "w_b_out": nrm(ks[17], (DEPTH, B_OUT, D), B_OUT ** -0.5),
        "w_c_out": nrm(ks[18], (DEPTH, C_WIDTH, D), C_WIDTH ** -0.5),
        "w_o": nrm(ks[19], (DEPTH, D, D), D ** -0.5),
        "mlp_norm_pre": 1.0 + nrm(ks[20], (DEPTH, D), 0.05),
        "mlp_norm_post": 1.0 + nrm(ks[21], (DEPTH, D), 0.05),
        "w_up": nrm(ks[22], (DEPTH, D, D_FF), D ** -0.5),
        "w_down": nrm(ks[23], (DEPTH, D_FF, D), D_FF ** -0.5),
    }


def reference(x, c, rel_bias, hgrn_lb_logits, w_ada, b_ada, mix_norm_pre, mix_norm_post,
              w_in, w_gate, b_gate, hgrn_norm_w, conv_w, conv_b, conv_ln_g, conv_ln_b,
              w_a_out, w_b_out, w_c_out, w_o, mlp_norm_pre, mlp_norm_post, w_up, w_down):
    Bsz, S, D = x.shape
    sm = jax.nn.softmax(hgrn_lb_logits.astype(jnp.float32), axis=0)
    lower_bounds = jnp.cumsum(sm, axis=0) - sm[0]
    c_act = jax.nn.silu(c)
    for l in range(DEPTH):
        mod = c_act @ w_ada[l] + b_ada[l]
        sh1, sc1, g1, sh2, sc2, g2 = jnp.split(mod[:, None, :], 6, axis=-1)

        h = rms_norm(x, mix_norm_pre[l]) * (1.0 + sc1) + sh1
        a_q, a_f, a_i, a_g, b_q, b_k, b_v, c_a, c_g = split_cols(h @ w_in[l], IN_SPLITS)
        ya = hgrn2_mixer(a_q, a_f, a_i, a_g, lower_bounds[l], hgrn_norm_w[l])
        yb = dilated_attention_mixer(b_q, b_k, b_v, rel_bias)
        yc = conv_module(c_a, c_g, conv_w[l], conv_b[l], conv_ln_g[l], conv_ln_b[l])
        gates = jax.nn.sigmoid(h @ w_gate[l] + b_gate[l]).reshape(Bsz, S, N_BRANCH, D)
        merged = (gates[:, :, 0] * (ya @ w_a_out[l])
                  + gates[:, :, 1] * (yb @ w_b_out[l])
                  + gates[:, :, 2] * (yc @ w_c_out[l]))
        x = x + g1 * rms_norm(merged @ w_o[l], mix_norm_post[l])

        h = rms_norm(x, mlp_norm_pre[l]) * (1.0 + sc2) + sh2
        y = jnp.square(jax.nn.relu(h @ w_up[l])) @ w_down[l]
        x = x + g2 * rms_norm(y, mlp_norm_post[l])
    return x
```

```python
import functools
import math

import numpy as np
import jax
import jax.numpy as jnp
from jax import lax
from jax.experimental import pallas as pl
from jax.experimental.pallas import tpu as pltpu

F32 = jnp.float32
BF16 = jnp.bfloat16

D_MODEL = 2048
DEPTH = 2
A_HEADS = 8
A_DIM = 128
A_WIDTH = A_HEADS * A_DIM
B_GROUPS = ((128, 1), (512, 4), (2048, 16))
B_SLOTS = 4
B_HEAD_DIM = 64
B_WIDTH = B_SLOTS * len(B_GROUPS) * B_HEAD_DIM
B_OUT = B_SLOTS * B_HEAD_DIM
B_BLOCK = 128
C_WIDTH = 768
C_KERNEL = 31
REL_BUCKETS = 32
REL_MAX_DIST = 2048
D_FF = 4 * D_MODEL
N_BRANCH = 3
IN_WIDTH = 4 * A_WIDTH + 3 * B_WIDTH + 2 * C_WIDTH
EPS = 1e-6
MASK_VALUE = -1e30
TINY = 1e-30

LANES = 128
MXU_WIDTH = 256
VMEM_LIMIT_BYTES = 56 * 1024 * 1024

HGRN_CHUNK = 128
HGRN_LEVELS = 7


def _cparams(*sem):
    return pltpu.CompilerParams(dimension_semantics=sem, vmem_limit_bytes=VMEM_LIMIT_BYTES)


def _dot(a, b):
    return jnp.dot(a, b, preferred_element_type=F32)


def _dot_nt(a, b):
    return lax.dot_general(a, b, (((1,), (1,)), ((), ())), preferred_element_type=F32)


def _dot_tn(a, b):
    return lax.dot_general(a, b, (((0,), (0,)), ((), ())), preferred_element_type=F32)


def _silu(x):
    return x * jax.nn.sigmoid(x)


def _rms(x, w):
    return x * lax.rsqrt(jnp.mean(x * x, axis=-1, keepdims=True) + EPS) * w


def _cast_kernel(x_ref, o_ref):
    o_ref[...] = x_ref[...].astype(o_ref.dtype)


def _cast_bf16(w):
    L, R, C = w.shape
    tr = next(t for t in (1024, 512, 256, 128, 64, 32, 16) if R % t == 0 and t * C * 4 <= 4 * 1024 * 1024)
    return pl.pallas_call(
        _cast_kernel,
        grid=(L, R // tr),
        in_specs=[pl.BlockSpec((None, tr, C), lambda l, i: (l, i, 0))],
        out_specs=pl.BlockSpec((None, tr, C), lambda l, i: (l, i, 0)),
        out_shape=jax.ShapeDtypeStruct(w.shape, BF16),
        compiler_params=_cparams("arbitrary", "arbitrary"),
        name="cast_bf16",
    )(w)


def _ada_kernel(c_ref, w_ref, b_ref, o_ref):
    ca = _silu(c_ref[...]).astype(BF16)
    o_ref[...] = _dot(ca, w_ref[...].astype(BF16)) + b_ref[...]


def _ada(c, w_ada, b_ada):
    Bsz, D = c.shape
    L, _, N = w_ada.shape
    rows = 16
    cp = jnp.zeros((rows, D), F32).at[:Bsz].set(c)
    tn = 1024
    out = pl.pallas_call(
        _ada_kernel,
        grid=(L, N // tn),
        in_specs=[pl.BlockSpec((rows, D), lambda l, j: (0, 0)),
                  pl.BlockSpec((None, D, tn), lambda l, j: (l, 0, j)),
                  pl.BlockSpec((None, 1, tn), lambda l, j: (l, 0, j))],
        out_specs=pl.BlockSpec((None, rows, tn), lambda l, j: (l, 0, j)),
        out_shape=jax.ShapeDtypeStruct((L, rows, N), F32),
        compiler_params=_cparams("arbitrary", "arbitrary"),
        name="ada_mod",
    )(cp, w_ada, b_ada.reshape(L, 1, N))
    return out[:, :Bsz].reshape(L, Bsz, 6, D)


def _prenorm_kernel(x_ref, nw_ref, mod_ref, o_ref, *, sh_row, sc_row):
    y = _rms(x_ref[...], nw_ref[...])
    o_ref[...] = (y * (1.0 + mod_ref[sc_row:sc_row + 1, :]) + mod_ref[sh_row:sh_row + 1, :]).astype(o_ref.dtype)


def _prenorm(x, nw, mod, sh_row, sc_row):
    Bsz, S, D = x.shape
    ts = min(S, 512)
    return pl.pallas_call(
        functools.partial(_prenorm_kernel, sh_row=sh_row, sc_row=sc_row),
        grid=(Bsz, S // ts),
        in_specs=[pl.BlockSpec((None, ts, D), lambda b, i: (b, i, 0)),
                  pl.BlockSpec((1, D), lambda b, i: (0, 0)),
                  pl.BlockSpec((None, 6, D), lambda b, i: (b, 0, 0))],
        out_specs=pl.BlockSpec((None, ts, D), lambda b, i: (b, i, 0)),
        out_shape=jax.ShapeDtypeStruct((Bsz, S, D), BF16),
        compiler_params=_cparams("arbitrary", "arbitrary"),
        name="prenorm",
    )(x, nw.reshape(1, D), mod)


def _proj_kernel(*refs, nb, bw, gate):
    h_ref = refs[0]
    w_refs = refs[1:1 + nb]
    b_ref = refs[1 + nb] if gate else None
    o_ref = refs[-2]
    wb_ref = refs[-1]

    @pl.when(pl.program_id(1) == 0)
    def _():
        for k in range(nb):
            wb_ref[k] = w_refs[k][...].astype(BF16)

    h = h_ref[...]
    for k in range(nb):
        acc = _dot(h, wb_ref[k])
        if gate:
            acc = jax.nn.sigmoid(acc + b_ref[:, k * bw:(k + 1) * bw])
        o_ref[:, k * bw:(k + 1) * bw] = acc.astype(o_ref.dtype)


def _proj(h2, w, layer, *, nb, bw, col0, nsteps, bias=None, stacked_out=False):
    T, D = h2.shape
    tm = min(T, 1024)
    tn = nb * bw
    assert col0 % bw == 0
    blk0 = col0 // bw
    in_specs = [pl.BlockSpec((tm, D), lambda j, i: (i, 0))]
    args = [h2]
    for k in range(nb):
        in_specs.append(pl.BlockSpec((None, D, bw), lambda j, i, k=k: (layer, 0, blk0 + j * nb + k)))
        args.append(w)
    if bias is not None:
        in_specs.append(pl.BlockSpec((None, 1, tn), lambda j, i: (layer, 0, j)))
        args.append(bias.reshape(bias.shape[0], 1, bias.shape[1]))
    if stacked_out:
        out_shape = jax.ShapeDtypeStruct((nsteps, T, tn), BF16)
        out_spec = pl.BlockSpec((None, tm, tn), lambda j, i: (j, i, 0))
    else:
        out_shape = jax.ShapeDtypeStruct((T, nsteps * tn), BF16)
        out_spec = pl.BlockSpec((tm, tn), lambda j, i: (i, j))
    return pl.pallas_call(
        functools.partial(_proj_kernel, nb=nb, bw=bw, gate=bias is not None),
        grid=(nsteps, T // tm),
        in_specs=in_specs,
        out_specs=out_spec,
        out_shape=out_shape,
        scratch_shapes=[pltpu.VMEM((nb, D, bw), BF16)],
        compiler_params=_cparams("arbitrary", "arbitrary"),
        name="in_proj",
    )(*args)


def _hgrn_constants():
    C = HGRN_CHUNK
    t = np.arange(C)[:, None]
    u = np.arange(C)[None, :]
    mats = []
    for li in range(HGRN_LEVELS + 1):
        w = 1 << li
        mats.append(((u // w == t // w) & (u <= t)).astype(np.float32))
    for li in range(1, HGRN_LEVELS + 1):
        w = 1 << li
        mats.append(((u // w == t // w) & (u > t)).astype(np.float32))
    masks = []
    for li in range(HGRN_LEVELS):
        w = 1 << li
        masks.append(((t // (2 * w) == u // (2 * w)) & ((t // w) % 2 == 1) & ((u // w) % 2 == 0)).astype(np.float32))
    masks.append((t == u).astype(np.float32))
    return np.concatenate(mats, axis=0), np.stack(masks, axis=0)


def _hgrn_kernel(q_ref, f_ref, i_ref, g_ref, lbl_ref, nw_ref, mst_ref, msk_ref, o_ref, st_ref, *, layer, hp, nchunk):
    C = HGRN_CHUNK
    NL = HGRN_LEVELS
    logits = lbl_ref[...]
    e = jnp.exp(logits - jnp.max(logits, axis=0, keepdims=True))
    sm = e / jnp.sum(e, axis=0, keepdims=True)
    lower = jnp.sum(sm[:layer + 1], axis=0, keepdims=True) - sm[0:1]
    lb = jnp.maximum(lower, 0.0)
    log_lb = jnp.log(lb + TINY)
    log_1m = jnp.log(1.0 - lb)
    one_m = 1.0 - lb
    nw = nw_ref[...]
    st_ref[...] = jnp.zeros_like(st_ref)

    def chunk(c, carry):
        r0 = pl.multiple_of(c * C, C)
        z = f_ref[pl.ds(r0, C), :].astype(F32)
        qp = q_ref[pl.ds(r0, C), :].astype(F32)
        v = i_ref[pl.ds(r0, C), :]
        og = g_ref[pl.ds(r0, C), :].astype(F32)
        sp = jnp.log(1.0 + jnp.exp(-jnp.abs(z)))
        ls = jnp.minimum(z, 0.0) - sp
        b2 = log_1m + ls
        lf = jnp.maximum(log_lb, b2) + jnp.log(1.0 + jnp.exp(-jnp.abs(log_lb - b2)))
        kk = one_m * jnp.exp(ls - z)
        q = _silu(qp)
        lf_hi = lf.astype(BF16)
        lf_lo = (lf - lf_hi.astype(F32)).astype(BF16)
        mst = mst_ref[...]
        ex = jnp.exp(_dot(mst, lf_hi) + _dot(mst, lf_lo))
        for hh in range(hp):
            sl = slice(hh * LANES, (hh + 1) * LANES)
            q_h, k_h, v_h = q[:, sl], kk[:, sl], v[:, sl]
            att = msk_ref[NL] * _dot_nt(q_h.astype(BF16), k_h.astype(BF16))
            for li in range(NL):
                qw = (q_h * ex[li * C:(li + 1) * C, sl]).astype(BF16)
                kw = k_h if li == 0 else k_h * ex[(NL + li) * C:(NL + li + 1) * C, sl]
                att = att + msk_ref[li] * _dot_nt(qw, kw.astype(BF16))
            o = _dot(att.astype(BF16), v_h)
            st = st_ref[hh]
            q_in = (q_h * ex[NL * C:(NL + 1) * C, sl]).astype(BF16)
            o = o + _dot_nt(q_in, st.astype(BF16))
            k_in = (k_h * ex[2 * NL * C:(2 * NL + 1) * C, sl]).astype(BF16)
            d_row = ex[(NL + 1) * C - 1:(NL + 1) * C, sl]
            st_ref[hh] = st * d_row + _dot_tn(v_h, k_in)
            y = _rms(o, nw) * _silu(og[:, sl])
            o_ref[pl.ds(r0, C), sl] = y.astype(o_ref.dtype)
        return carry

    lax.fori_loop(0, nchunk, chunk, 0)


def _hgrn(a_proj, lb_logits, norm_w, layer, hp=2):
    Bsz, S, _ = a_proj.shape
    mst, msk = _hgrn_constants()
    bw = hp * A_DIM
    nhb = A_WIDTH // bw
    specs = [pl.BlockSpec((None, S, bw), lambda b, h, k=k: (b, 0, k * nhb + h)) for k in range(4)]
    return pl.pallas_call(
        functools.partial(_hgrn_kernel, layer=layer, hp=hp, nchunk=S // HGRN_CHUNK),
        grid=(Bsz, nhb),
        in_specs=specs + [
            pl.BlockSpec((DEPTH, bw), lambda b, h: (0, h)),
            pl.BlockSpec((1, A_DIM), lambda b, h: (0, 0)),
            pl.BlockSpec(mst.shape, lambda b, h: (0, 0)),
            pl.BlockSpec(msk.shape, lambda b, h: (0, 0, 0)),
        ],
        out_specs=pl.BlockSpec((None, S, bw), lambda b, h: (b, 0, h)),
        out_shape=jax.ShapeDtypeStruct((Bsz, S, A_WIDTH), BF16),
        scratch_shapes=[pltpu.VMEM((hp, A_DIM, A_DIM), F32)],
        compiler_params=_cparams("arbitrary", "arbitrary"),
        name="hgrn2",
    )(a_proj, a_proj, a_proj, a_proj, lb_logits, norm_w.reshape(1, A_DIM),
      jnp.asarray(mst, BF16), jnp.asarray(msk, F32))


def _bias_indices():
    Q = B_BLOCK
    steps = np.arange(Q)[:, None] + Q - np.arange(2 * Q)[None, :]
    valid = (steps >= 0) & (steps <= Q)
    exact = REL_BUCKETS // 2
    idx = []
    for _, dil in B_GROUPS:
        dist = np.clip(steps, 0, Q) * dil
        d = np.maximum(dist, 1).astype(np.float32)
        large = exact + (np.log(d / np.float32(exact)) / np.float32(math.log(REL_MAX_DIST / exact))
                         * np.float32(REL_BUCKETS - exact)).astype(np.int32)
        idx.append(np.where(dist < exact, dist, np.clip(large, exact, REL_BUCKETS - 1)))
    has_prev = np.stack([valid & (np.arange(2 * Q)[None, :] >= Q), valid], axis=0)
    return np.stack(idx, axis=0).astype(np.int32), has_prev


def _bias_tables(rel_bias):
    idx, ok = _bias_indices()
    tabs = []
    for g in range(len(B_GROUPS)):
        t = rel_bias[idx[g]][:, :, g * B_SLOTS:(g + 1) * B_SLOTS].astype(F32)
        t = jnp.transpose(t, (2, 0, 1))[:, None]
        tabs.append(jnp.where(jnp.asarray(ok)[None], t, MASK_VALUE))
    return jnp.stack(tabs, axis=0)


def _attn_kernel(*refs, seq):
    ng = len(B_GROUPS)
    qkv = refs[:3 * ng]
    bias_ref = refs[3 * ng]
    o_ref = refs[3 * ng + 1]
    tmp_ref, qd_ref, kd_ref, vd_ref, m_ref, l_ref, a_ref, mr_ref, lr_ref, ar_ref = refs[3 * ng + 2:]
    Q = B_BLOCK
    lane = lax.broadcasted_iota(jnp.int32, (Q, LANES), 1)
    low = lane < B_HEAD_DIM
    kd_ref[0:Q, :] = jnp.zeros((Q, LANES), F32)
    vd_ref[0:Q, :] = jnp.zeros((Q, LANES), F32)

    for g, (_, dil) in enumerate(B_GROUPS):
        L = seq // dil
        nblk = L // Q
        for src, dst in zip(qkv[3 * g:3 * g + 3], (qd_ref, kd_ref, vd_ref)):
            if dil == 1:
                dst[Q:Q + seq, :] = src[...].astype(F32)
            else:
                tmp_ref[...] = src[...].astype(F32)
                for r in range(dil):
                    dst[Q + r * L:Q + (r + 1) * L, :] = tmp_ref[pl.ds(r, L, stride=dil), :]
        mo, lo, ao = (m_ref, l_ref, a_ref) if dil == 1 else (mr_ref, lr_ref, ar_ref)

        def unit(u, carry, g=g, nblk=nblk, mo=mo, lo=lo, ao=ao):
            base = pl.multiple_of(u * Q, Q)
            var = jnp.where(u % nblk == 0, 0, 1)
            qb = qd_ref[pl.ds(base + Q, Q), :] * (B_HEAD_DIM ** -0.5)
            kb = kd_ref[pl.ds(base, 2 * Q), :].astype(BF16)
            vb = vd_ref[pl.ds(base, 2 * Q), :].astype(BF16)
            ms, ls, pv = [], [], []
            for s in range(2):
                qs = jnp.where(low if s == 0 else ~low, qb, 0.0).astype(BF16)
                sc = _dot_nt(qs, kb) + bias_ref[g, s, var]
                m = jnp.max(sc, axis=-1, keepdims=True)
                p = jnp.exp(sc - m)
                ms.append(m)
                ls.append(jnp.sum(p, axis=-1, keepdims=True))
                pv.append(_dot(p.astype(BF16), vb))
            mo[pl.ds(base, Q), :] = jnp.where(low, ms[0], ms[1])
            lo[pl.ds(base, Q), :] = jnp.where(low, ls[0], ls[1])
            ao[pl.ds(base, Q), :] = jnp.where(low, pv[0], pv[1])
            return carry

        lax.fori_loop(0, seq // Q, unit, 0)

        if dil > 1:
            for r in range(dil):
                tok = pl.ds(r, L, stride=dil)
                res = slice(r * L, (r + 1) * L)
                m_old, m_new = m_ref[tok, :], mr_ref[res, :]
                mx = jnp.maximum(m_old, m_new)
                wa, wb = jnp.exp(m_old - mx), jnp.exp(m_new - mx)
                m_ref[tok, :] = mx
                l_ref[tok, :] = wa * l_ref[tok, :] + wb * lr_ref[res, :]
                a_ref[tok, :] = wa * a_ref[tok, :] + wb * ar_ref[res, :]

    o_ref[...] = (a_ref[...] / l_ref[...]).astype(o_ref.dtype)


def _attn(bqkv, rel_bias):
    _, Bsz, S, _ = bqkv.shape
    ng = len(B_GROUPS)
    pairs = B_SLOTS // 2
    tabs = _bias_tables(rel_bias)
    in_specs, args = [], []
    for g in range(ng):
        for a in range(3):
            in_specs.append(pl.BlockSpec((None, None, S, LANES), lambda b, p, a=a, g=g: (a, b, 0, g * pairs + p)))
            args.append(bqkv)
    in_specs.append(pl.BlockSpec((ng, 2, 2, B_BLOCK, 2 * B_BLOCK), lambda b, p: (0, p, 0, 0, 0)))
    big = pltpu.VMEM((S, LANES), F32)
    pad = pltpu.VMEM((S + B_BLOCK, LANES), F32)
    return pl.pallas_call(
        functools.partial(_attn_kernel, seq=S),
        grid=(Bsz, pairs),
        in_specs=in_specs,
        out_specs=pl.BlockSpec((None, S, LANES), lambda b, p: (b, 0, p)),
        out_shape=jax.ShapeDtypeStruct((Bsz, S, B_OUT), BF16),
        scratch_shapes=[big, pad, pad, pad, big, big, big, big, big, big],
        compiler_params=_cparams("arbitrary", "arbitrary"),
        name="dilated_attn",
    )(*args, tabs)


CONV_HALO = 32
CONV_ROWS = 64


def _conv_kernel(a_ref, g_ref, w_ref, cb_ref, lg_ref, lbias_ref, o_ref, u_ref, *, ts):
    @pl.when(pl.program_id(1) == 0)
    def _():
        u_ref[0:CONV_HALO, :] = jnp.zeros((CONV_HALO, C_WIDTH), F32)

    u_ref[CONV_HALO:CONV_HALO + ts, :] = a_ref[...].astype(F32) * jax.nn.sigmoid(g_ref[...].astype(F32))
    off = CONV_HALO - (C_KERNEL - 1)

    def rows(c, carry):
        r0 = pl.multiple_of(c * CONV_ROWS, CONV_ROWS)
        nwin = CONV_ROWS + CONV_HALO
        win = u_ref[pl.ds(r0, nwin), :]
        acc = jnp.zeros((CONV_ROWS, C_WIDTH), F32)
        for s in range(8):
            taps = [j for j in range(C_KERNEL) if (off + j) % 8 == s]
            sh = win if s == 0 else pltpu.roll(win, nwin - s, 0)
            for j in taps:
                a0 = off + j - s
                acc = acc + w_ref[j:j + 1, :] * sh[a0:a0 + CONV_ROWS, :]
        y = acc + cb_ref[...]
        mu = jnp.mean(y, axis=-1, keepdims=True)
        yc = y - mu
        var = jnp.mean(yc * yc, axis=-1, keepdims=True)
        o_ref[pl.ds(r0, CONV_ROWS), :] = _silu(yc * lax.rsqrt(var + EPS) * lg_ref[...] + lbias_ref[...]).astype(o_ref.dtype)
        return carry

    lax.fori_loop(0, ts // CONV_ROWS, rows, 0)
    u_ref[0:CONV_HALO, :] = u_ref[ts:ts + CONV_HALO, :]


def _conv(ca, cg, conv_w, conv_b, ln_g, ln_b):
    (arr_a, ia), (arr_g, ig) = ca, cg
    _, Bsz, S, C = arr_a.shape
    ts = min(S, 512)
    wpad = jnp.zeros((CONV_HALO, C), F32).at[:C_KERNEL].set(conv_w)
    vec = lambda v: v.reshape(1, C)
    vspec = pl.BlockSpec((1, C), lambda b, i: (0, 0))
    return pl.pallas_call(
        functools.partial(_conv_kernel, ts=ts),
        grid=(Bsz, S // ts),
        in_specs=[pl.BlockSpec((None, None, ts, C), lambda b, i: (ia, b, i, 0)),
                  pl.BlockSpec((None, None, ts, C), lambda b, i: (ig, b, i, 0)),
                  pl.BlockSpec((CONV_HALO, C), lambda b, i: (0, 0)),
                  vspec, vspec, vspec],
        out_specs=pl.BlockSpec((None, ts, C), lambda b, i: (b, i, 0)),
        out_shape=jax.ShapeDtypeStruct((Bsz, S, C), BF16),
        scratch_shapes=[pltpu.VMEM((CONV_HALO + ts, C), F32)],
        compiler_params=_cparams("arbitrary", "arbitrary"),
        name="conv_module",
    )(arr_a, arr_g, wpad, vec(conv_b), vec(ln_g), vec(ln_b))


def _mix_out_kernel(x_ref, ya_ref, yb_ref, yc_ref, g0_ref, g1_ref, g2_ref, wa_ref, wb_ref, wc_ref, wo_ref,
                    nw_ref, mod_ref, o_ref, acc_ref, *, gate_row):
    j = pl.program_id(2)

    @pl.when(j == 0)
    def _():
        acc_ref[...] = jnp.zeros_like(acc_ref)

    m = (g0_ref[...].astype(F32) * _dot(ya_ref[...], wa_ref[...])
         + g1_ref[...].astype(F32) * _dot(yb_ref[...], wb_ref[...])
         + g2_ref[...].astype(F32) * _dot(yc_ref[...], wc_ref[...]))
    acc_ref[...] += _dot(m.astype(BF16), wo_ref[...])

    @pl.when(j == pl.num_programs(2) - 1)
    def _():
        o_ref[...] = x_ref[...] + mod_ref[gate_row:gate_row + 1, :] * _rms(acc_ref[...], nw_ref[...])


def _mix_out(x, ya, yb, yc, gates, wa, wb, wc, wo, nw, mod, layer, gate_row):
    Bsz, S, D = x.shape
    tm = min(S, 512)
    tn = 512
    nj = D // tn
    tok = lambda width: pl.BlockSpec((None, tm, width), lambda b, i, j: (b, i, 0))
    gspec = lambda k: pl.BlockSpec((None, tm, tn), lambda b, i, j, k=k: (b, i, k * nj + j))
    wspec = lambda K: pl.BlockSpec((None, K, tn), lambda b, i, j: (layer, 0, j))
    return pl.pallas_call(
        functools.partial(_mix_out_kernel, gate_row=gate_row),
        grid=(Bsz, S // tm, nj),
        in_specs=[tok(D), tok(ya.shape[-1]), tok(yb.shape[-1]), tok(yc.shape[-1]),
                  gspec(0), gspec(1), gspec(2),
                  wspec(wa.shape[1]), wspec(wb.shape[1]), wspec(wc.shape[1]),
                  pl.BlockSpec((None, tn, D), lambda b, i, j: (layer, j, 0)),
                  pl.BlockSpec((1, D), lambda b, i, j: (0, 0)),
                  pl.BlockSpec((None, 6, D), lambda b, i, j: (b, 0, 0))],
        out_specs=tok(D),
        out_shape=jax.ShapeDtypeStruct((Bsz, S, D), F32),
        scratch_shapes=[pltpu.VMEM((tm, D), F32)],
        compiler_params=_cparams("arbitrary", "arbitrary", "arbitrary"),
        name="mix_out",
    )(x, ya, yb, yc, gates, gates, gates, wa, wb, wc, wo, nw.reshape(1, D), mod)


def _mlp_kernel(x_ref, wu_ref, wd_ref, npre_ref, npost_ref, mod_ref, o_ref, h_ref, acc_ref, *, sh_row, sc_row, gate_row):
    j = pl.program_id(2)

    @pl.when(j == 0)
    def _():
        y = _rms(x_ref[...], npre_ref[...])
        h_ref[...] = (y * (1.0 + mod_ref[sc_row:sc_row + 1, :]) + mod_ref[sh_row:sh_row + 1, :]).astype(h_ref.dtype)
        acc_ref[...] = jnp.zeros_like(acc_ref)

    u = jnp.maximum(_dot(h_ref[...], wu_ref[...]), 0.0)
    acc_ref[...] += _dot((u * u).astype(BF16), wd_ref[...])

    @pl.when(j == pl.num_programs(2) - 1)
    def _():
        o_ref[...] = x_ref[...] + mod_ref[gate_row:gate_row + 1, :] * _rms(acc_ref[...], npost_ref[...])


def _mlp(x, wu, wd, npre, npost, mod, layer):
    Bsz, S, D = x.shape
    FF = wu.shape[-1]
    tm = min(S, 512)
    tf = min(FF, 1024)
    tok = pl.BlockSpec((None, tm, D), lambda b, i, j: (b, i, 0))
    vspec = pl.BlockSpec((1, D), lambda b, i, j: (0, 0))
    return pl.pallas_call(
        functools.partial(_mlp_kernel, sh_row=3, sc_row=4, gate_row=5),
        grid=(Bsz, S // tm, FF // tf),
        in_specs=[tok,
                  pl.BlockSpec((None, D, tf), lambda b, i, j: (layer, 0, j)),
                  pl.BlockSpec((None, tf, D), lambda b, i, j: (layer, j, 0)),
                  vspec, vspec,
                  pl.BlockSpec((None, 6, D), lambda b, i, j: (b, 0, 0))],
        out_specs=tok,
        out_shape=jax.ShapeDtypeStruct((Bsz, S, D), F32),
        scratch_shapes=[pltpu.VMEM((tm, D), BF16), pltpu.VMEM((tm, D), F32)],
        compiler_params=_cparams("arbitrary", "arbitrary", "arbitrary"),
        name="mlp",
    )(x, wu, wd, npre.reshape(1, D), npost.reshape(1, D), mod)


def kernel(x, c, rel_bias, hgrn_lb_logits, w_ada, b_ada, mix_norm_pre, mix_norm_post, w_in, w_gate, b_gate,
           hgrn_norm_w, conv_w, conv_b, conv_ln_g, conv_ln_b, w_a_out, w_b_out, w_c_out, w_o, mlp_norm_pre,
           mlp_norm_post, w_up, w_down):
    Bsz, S, D = x.shape
    T = Bsz * S
    mod = _ada(c, w_ada, b_ada)
    wa, wb, wc, wo = _cast_bf16(w_a_out), _cast_bf16(w_b_out), _cast_bf16(w_c_out), _cast_bf16(w_o)
    wu, wd = _cast_bf16(w_up), _cast_bf16(w_down)
    lb_logits = hgrn_lb_logits.astype(F32)
    for l in range(DEPTH):
        h = _prenorm(x, mix_norm_pre[l], mod[l], 0, 1).reshape(T, D)
        a_proj = _proj(h, w_in, l, nb=1, bw=1024, col0=0, nsteps=4)
        bc = _proj(h, w_in, l, nb=3, bw=256, col0=4 * A_WIDTH, nsteps=5, stacked_out=True)
        gates = _proj(h, w_gate, l, nb=1, bw=1024, col0=0, nsteps=N_BRANCH * D // 1024, bias=b_gate)
        bc = bc.reshape(5, Bsz, S, C_WIDTH)
        ya = _hgrn(a_proj.reshape(Bsz, S, 4 * A_WIDTH), lb_logits, hgrn_norm_w[l], l)
        yb = _attn(bc, rel_bias)
        yc = _conv((bc, 3), (bc, 4), conv_w[l], conv_b[l], conv_ln_g[l], conv_ln_b[l])
        x = _mix_out(x, ya, yb, yc, gates.reshape(Bsz, S, N_BRANCH * D), wa, wb, wc, wo,
                     mix_norm_post[l], mod[l], l, 2)
        x = _mlp(x, wu, wd, mlp_norm_pre[l], mlp_norm_post[l], mod[l], l)
    return x
```

```python
import functools
import math

import numpy as np
import jax
import jax.numpy as jnp
from jax import lax
from jax.experimental import pallas as pl
from jax.experimental.pallas import tpu as pltpu

F32 = jnp.float32
BF16 = jnp.bfloat16

D_MODEL = 2048
DEPTH = 2
A_HEADS = 8
A_DIM = 128
A_WIDTH = A_HEADS * A_DIM
B_GROUPS = ((128, 1), (512, 4), (2048, 16))
B_SLOTS = 4
B_HEAD_DIM = 64
B_WIDTH = B_SLOTS * len(B_GROUPS) * B_HEAD_DIM
B_OUT = B_SLOTS * B_HEAD_DIM
B_BLOCK = 128
C_WIDTH = 768
C_KERNEL = 31
REL_BUCKETS = 32
REL_MAX_DIST = 2048
D_FF = 4 * D_MODEL
N_BRANCH = 3
IN_WIDTH = 4 * A_WIDTH + 3 * B_WIDTH + 2 * C_WIDTH
EPS = 1e-6
MASK_VALUE = -1e30
TINY = 1e-30
LOG2E = math.log2(math.e)

LANES = 128
MXU_WIDTH = 256
VMEM_LIMIT_BYTES = 56 * 1024 * 1024

HGRN_CHUNK = 128
HGRN_LEVELS = 7


def _cparams(*sem):
    return pltpu.CompilerParams(dimension_semantics=sem, vmem_limit_bytes=VMEM_LIMIT_BYTES)


def _dot(a, b):
    return jnp.dot(a, b, preferred_element_type=F32)


def _dot_nt(a, b):
    return lax.dot_general(a, b, (((1,), (1,)), ((), ())), preferred_element_type=F32)


def _dot_tn(a, b):
    return lax.dot_general(a, b, (((0,), (0,)), ((), ())), preferred_element_type=F32)


def _silu(x):
    return x * jax.nn.sigmoid(x)


def _rms(x, w):
    return x * lax.rsqrt(jnp.mean(x * x, axis=-1, keepdims=True) + EPS) * w


def _cast_kernel(x_ref, o_ref):
    o_ref[...] = x_ref[...].astype(o_ref.dtype)


def _cast_bf16(w):
    L, R, C = w.shape
    tr = next(t for t in (1024, 512, 256, 128, 64, 32, 16) if R % t == 0 and t * C * 4 <= 4 * 1024 * 1024)
    return pl.pallas_call(
        _cast_kernel,
        grid=(L, R // tr),
        in_specs=[pl.BlockSpec((None, tr, C), lambda l, i: (l, i, 0))],
        out_specs=pl.BlockSpec((None, tr, C), lambda l, i: (l, i, 0)),
        out_shape=jax.ShapeDtypeStruct(w.shape, BF16),
        compiler_params=_cparams("arbitrary", "arbitrary"),
        name="cast_bf16",
    )(w)


def _ada_kernel(c_ref, w_ref, b_ref, o_ref):
    ca = _silu(c_ref[...]).astype(BF16)
    o_ref[...] = _dot(ca, w_ref[...].astype(BF16)) + b_ref[...]


def _ada(c, w_ada, b_ada):
    Bsz, D = c.shape
    L, _, N = w_ada.shape
    rows = 16
    cp = jnp.zeros((rows, D), F32).at[:Bsz].set(c)
    tn = 1024
    out = pl.pallas_call(
        _ada_kernel,
        grid=(L, N // tn),
        in_specs=[pl.BlockSpec((rows, D), lambda l, j: (0, 0)),
                  pl.BlockSpec((None, D, tn), lambda l, j: (l, 0, j)),
                  pl.BlockSpec((None, 1, tn), lambda l, j: (l, 0, j))],
        out_specs=pl.BlockSpec((None, rows, tn), lambda l, j: (l, 0, j)),
        out_shape=jax.ShapeDtypeStruct((L, rows, N), F32),
        compiler_params=_cparams("arbitrary", "arbitrary"),
        name="ada_mod",
    )(cp, w_ada, b_ada.reshape(L, 1, N))
    return out[:, :Bsz].reshape(L, Bsz, 6, D)


def _prenorm_kernel(x_ref, nw_ref, mod_ref, o_ref, *, sh_row, sc_row):
    y = _rms(x_ref[...], nw_ref[...])
    o_ref[...] = (y * (1.0 + mod_ref[sc_row:sc_row + 1, :]) + mod_ref[sh_row:sh_row + 1, :]).astype(o_ref.dtype)


def _prenorm(x, nw, mod, sh_row, sc_row):
    Bsz, S, D = x.shape
    ts = min(S, 512)
    return pl.pallas_call(
        functools.partial(_prenorm_kernel, sh_row=sh_row, sc_row=sc_row),
        grid=(Bsz, S // ts),
        in_specs=[pl.BlockSpec((None, ts, D), lambda b, i: (b, i, 0)),
                  pl.BlockSpec((1, D), lambda b, i: (0, 0)),
                  pl.BlockSpec((None, 6, D), lambda b, i: (b, 0, 0))],
        out_specs=pl.BlockSpec((None, ts, D), lambda b, i: (b, i, 0)),
        out_shape=jax.ShapeDtypeStruct((Bsz, S, D), BF16),
        compiler_params=_cparams("arbitrary", "arbitrary"),
        name="prenorm",
    )(x, nw.reshape(1, D), mod)


def _proj_kernel(*refs, nb, bw, gate):
    h_ref = refs[0]
    w_refs = refs[1:1 + nb]
    b_ref = refs[1 + nb] if gate else None
    o_ref = refs[-2]
    wb_ref = refs[-1]

    @pl.when(pl.program_id(1) == 0)
    def _():
        for k in range(nb):
            wb_ref[k] = w_refs[k][...].astype(BF16)

    h = h_ref[...]
    for k in range(nb):
        acc = _dot(h, wb_ref[k])
        if gate:
            acc = jax.nn.sigmoid(acc + b_ref[:, k * bw:(k + 1) * bw])
        o_ref[:, k * bw:(k + 1) * bw] = acc.astype(o_ref.dtype)


def _proj(h2, w, layer, *, nb, bw, col0, nsteps, bias=None, stacked_out=False):
    T, D = h2.shape
    tm = min(T, 1024)
    tn = nb * bw
    assert col0 % bw == 0
    blk0 = col0 // bw
    in_specs = [pl.BlockSpec((tm, D), lambda j, i: (i, 0))]
    args = [h2]
    for k in range(nb):
        in_specs.append(pl.BlockSpec((None, D, bw), lambda j, i, k=k: (layer, 0, blk0 + j * nb + k)))
        args.append(w)
    if bias is not None:
        in_specs.append(pl.BlockSpec((None, 1, tn), lambda j, i: (layer, 0, j)))
        args.append(bias.reshape(bias.shape[0], 1, bias.shape[1]))
    if stacked_out:
        out_shape = jax.ShapeDtypeStruct((nsteps, T, tn), BF16)
        out_spec = pl.BlockSpec((None, tm, tn), lambda j, i: (j, i, 0))
    else:
        out_shape = jax.ShapeDtypeStruct((T, nsteps * tn), BF16)
        out_spec = pl.BlockSpec((tm, tn), lambda j, i: (i, j))
    return pl.pallas_call(
        functools.partial(_proj_kernel, nb=nb, bw=bw, gate=bias is not None),
        grid=(nsteps, T // tm),
        in_specs=in_specs,
        out_specs=out_spec,
        out_shape=out_shape,
        scratch_shapes=[pltpu.VMEM((nb, D, bw), BF16)],
        compiler_params=_cparams("arbitrary", "arbitrary"),
        name="in_proj",
    )(*args)


def _hgrn_constants():
    C = HGRN_CHUNK
    t = np.arange(C)[:, None]
    u = np.arange(C)[None, :]
    mats, masks = [], []
    for li in range(HGRN_LEVELS):
        w = 1 << li
        same = (u // (2 * w)) == (t // (2 * w))
        mid = (t // (2 * w)) * (2 * w) + w - 1
        upper = (t // w) % 2 == 1
        mats.append(np.where(upper, same & (u > mid) & (u <= t), same & (u > t) & (u <= mid)))
        masks.append(same & upper & ((u // w) % 2 == 0))
    mats.append(u <= t)
    mats.append(u > t)
    m = np.concatenate(mats, axis=0).astype(np.float32)
    return np.concatenate([m, m], axis=1), np.stack(masks, axis=0).astype(np.float32)


def _hgrn_kernel(q_ref, f_ref, i_ref, g_ref, lbl_ref, nw_ref, mst_ref, msk_ref, o_ref, st_ref, *, layer, hp, nchunk):
    C = HGRN_CHUNK
    NL = HGRN_LEVELS
    logits = lbl_ref[...]
    e = jnp.exp(logits - jnp.max(logits, axis=0, keepdims=True))
    sm = e / jnp.sum(e, axis=0, keepdims=True)
    lower = jnp.sum(sm[:layer + 1], axis=0, keepdims=True) - sm[0:1]
    lb = jnp.maximum(lower, 0.0)
    log_lb = jnp.log(lb + TINY)
    log_1m = jnp.log(1.0 - lb)
    one_m = 1.0 - lb
    nw = nw_ref[...]
    st_ref[...] = jnp.zeros_like(st_ref)

    def chunk(c, carry):
        r0 = pl.multiple_of(c * C, C)
        z = f_ref[pl.ds(r0, C), :].astype(F32)
        qp = q_ref[pl.ds(r0, C), :].astype(F32)
        v = i_ref[pl.ds(r0, C), :]
        og = g_ref[pl.ds(r0, C), :].astype(F32)
        sp = jnp.log(1.0 + jnp.exp(-jnp.abs(z)))
        ls = jnp.minimum(z, 0.0) - sp
        b2 = log_1m + ls
        lf = jnp.maximum(log_lb, b2) + jnp.log(1.0 + jnp.exp(-jnp.abs(log_lb - b2)))
        kk = one_m * jnp.exp(ls - z)
        q = _silu(qp)
        lf2 = lf * LOG2E
        lf_hi = lf2.astype(BF16)
        lf_lo = (lf2 - lf_hi.astype(F32)).astype(BF16)
        ex = jnp.exp2(_dot(mst_ref[...], jnp.concatenate([lf_hi, lf_lo], axis=0)))
        for hh in range(hp):
            sl = slice(hh * LANES, (hh + 1) * LANES)
            q_h, k_h, v_h = q[:, sl], kk[:, sl], v[:, sl]
            att = None
            for li in range(NL):
                e_l = ex[li * C:(li + 1) * C, sl]
                s_l = msk_ref[li] * _dot_nt((q_h * e_l).astype(BF16), (k_h * e_l).astype(BF16))
                att = s_l if att is None else att + s_l
            diag = jnp.sum(q_h * k_h, axis=-1, keepdims=True)
            o = _dot(att.astype(BF16), v_h) + diag * v_h.astype(F32)
            st = st_ref[hh]
            q_in = (q_h * ex[NL * C:(NL + 1) * C, sl]).astype(BF16)
            o = o + _dot_nt(q_in, st.astype(BF16))
            k_in = (k_h * ex[(NL + 1) * C:(NL + 2) * C, sl]).astype(BF16)
            d_row = ex[(NL + 1) * C - 1:(NL + 1) * C, sl]
            st_ref[hh] = st * d_row + _dot_tn(v_h, k_in)
            y = _rms(o, nw) * _silu(og[:, sl])
            o_ref[pl.ds(r0, C), sl] = y.astype(o_ref.dtype)
        return carry

    lax.fori_loop(0, nchunk, chunk, 0, unroll=2)


def _hgrn(a_proj, lb_logits, norm_w, layer, hp=4):
    Bsz, S, _ = a_proj.shape
    mst, msk = _hgrn_constants()
    bw = hp * A_DIM
    nhb = A_WIDTH // bw
    specs = [pl.BlockSpec((None, S, bw), lambda b, h, k=k: (b, 0, k * nhb + h)) for k in range(4)]
    return pl.pallas_call(
        functools.partial(_hgrn_kernel, layer=layer, hp=hp, nchunk=S // HGRN_CHUNK),
        grid=(Bsz, nhb),
        in_specs=specs + [
            pl.BlockSpec((DEPTH, bw), lambda b, h: (0, h)),
            pl.BlockSpec((1, A_DIM), lambda b, h: (0, 0)),
            pl.BlockSpec(mst.shape, lambda b, h: (0, 0)),
            pl.BlockSpec(msk.shape, lambda b, h: (0, 0, 0)),
        ],
        out_specs=pl.BlockSpec((None, S, bw), lambda b, h: (b, 0, h)),
        out_shape=jax.ShapeDtypeStruct((Bsz, S, A_WIDTH), BF16),
        scratch_shapes=[pltpu.VMEM((hp, A_DIM, A_DIM), F32)],
        compiler_params=_cparams("arbitrary", "arbitrary"),
        name="hgrn2",
    )(a_proj, a_proj, a_proj, a_proj, lb_logits, norm_w.reshape(1, A_DIM),
      jnp.asarray(mst, BF16), jnp.asarray(msk, F32))


def _bias_indices():
    Q = B_BLOCK
    steps = np.arange(Q)[:, None] + Q - np.arange(2 * Q)[None, :]
    valid = (steps >= 0) & (steps <= Q)
    exact = REL_BUCKETS // 2
    idx = []
    for _, dil in B_GROUPS:
        dist = np.clip(steps, 0, Q) * dil
        d = np.maximum(dist, 1).astype(np.float32)
        large = exact + (np.log(d / np.float32(exact)) / np.float32(math.log(REL_MAX_DIST / exact))
                         * np.float32(REL_BUCKETS - exact)).astype(np.int32)
        idx.append(np.where(dist < exact, dist, np.clip(large, exact, REL_BUCKETS - 1)))
    return np.stack(idx, axis=0).astype(np.int32)


def _build_bias_tables(idx_ref, rel_ref, tab_ref, pair):
    Q = B_BLOCK
    qi = lax.broadcasted_iota(jnp.int32, (Q, 2 * Q), 0)
    ki = lax.broadcasted_iota(jnp.int32, (Q, 2 * Q), 1)
    steps = qi + Q - ki
    valid = (steps >= 0) & (steps <= Q)
    valid_first = valid & (ki >= Q)
    for g in range(len(B_GROUPS)):
        idx = idx_ref[g]
        for s in range(2):
            head = g * B_SLOTS + 2 * pair + s
            bias = jnp.zeros((Q, 2 * Q), F32)
            for b in range(REL_BUCKETS):
                bias = jnp.where(idx == b, rel_ref[b, head], bias)
            bias = bias * LOG2E
            tab_ref[g, s, 0] = jnp.where(valid_first, bias, MASK_VALUE)
            tab_ref[g, s, 1] = jnp.where(valid, bias, MASK_VALUE)


ATTN_UNROLL = 4
ATTN_SUB = 4


def _attn_kernel(*refs, seq):
    ng = len(B_GROUPS)
    qkv = refs[:3 * ng]
    idx_ref, rel_ref, o_ref = refs[3 * ng:3 * ng + 3]
    tab_ref, tmp_ref, tmp2_ref, qd_ref, kd_ref, vd_ref = refs[3 * ng + 3:3 * ng + 9]
    res = refs[3 * ng + 9:]
    Q = B_BLOCK
    lane = lax.broadcasted_iota(jnp.int32, (Q, LANES), 1)
    low = lane < B_HEAD_DIM
    _build_bias_tables(idx_ref, rel_ref, tab_ref, pl.program_id(1))
    kd_ref[0:Q, :] = jnp.zeros((Q, LANES), F32)
    vd_ref[0:Q, :] = jnp.zeros((Q, LANES), F32)
    quarter = seq // ATTN_SUB

    for g, (_, dil) in enumerate(B_GROUPS):
        nblk = seq // dil // Q
        for src, dst in zip(qkv[3 * g:3 * g + 3], (qd_ref, kd_ref, vd_ref)):
            if dil == 1:
                dst[Q:Q + seq, :] = src[...].astype(F32)
                continue
            tmp_ref[...] = src[...].astype(F32)
            mid = dst if dil == ATTN_SUB else tmp2_ref
            off = Q if dil == ATTN_SUB else 0
            for r1 in range(ATTN_SUB):
                mid[off + r1 * quarter:off + (r1 + 1) * quarter, :] = tmp_ref[pl.ds(r1, quarter, stride=ATTN_SUB), :]
            if dil != ATTN_SUB:
                for r in range(ATTN_SUB * ATTN_SUB):
                    r1, r2 = divmod(r, ATTN_SUB)
                    dst[Q + r * Q:Q + (r + 1) * Q, :] = tmp2_ref[pl.ds(r1 * quarter + r2, Q, stride=ATTN_SUB), :]
        mo, lo, ao = res[3 * g:3 * g + 3]

        def units(it, carry, g=g, nblk=nblk, mo=mo, lo=lo, ao=ao):
            for uu in range(ATTN_UNROLL):
                u = it * ATTN_UNROLL + uu
                base = pl.multiple_of(u * Q, Q)
                var = jnp.where(u % nblk == 0, 0, 1)
                qb = qd_ref[pl.ds(base + Q, Q), :] * (B_HEAD_DIM ** -0.5 * LOG2E)
                kb = kd_ref[pl.ds(base, 2 * Q), :].astype(BF16)
                vb = vd_ref[pl.ds(base, 2 * Q), :].astype(BF16)
                ms, ls, pv = [], [], []
                for s in range(2):
                    qs = jnp.where(low if s == 0 else ~low, qb, 0.0).astype(BF16)
                    sc = _dot_nt(qs, kb) + tab_ref[g, s, var]
                    m = jnp.max(sc, axis=-1, keepdims=True)
                    p = jnp.exp2(sc - m)
                    ms.append(m)
                    ls.append(jnp.sum(p, axis=-1, keepdims=True))
                    pv.append(_dot(p.astype(BF16), vb))
                mo[pl.ds(base, Q), :] = jnp.where(low, ms[0], ms[1])
                lo[pl.ds(base, Q), :] = jnp.where(low, ls[0], ls[1])
                ao[pl.ds(base, Q), :] = jnp.where(low, pv[0], pv[1])
            return carry

        lax.fori_loop(0, seq // Q // ATTN_UNROLL, units, 0)

    def merge(dst, other):
        m_a, m_b = dst[0][...], other[0][...]
        mx = jnp.maximum(m_a, m_b)
        wa, wb = jnp.exp2(m_a - mx), jnp.exp2(m_b - mx)
        dst[0][...] = mx
        dst[1][...] = wa * dst[1][...] + wb * other[1][...]
        dst[2][...] = wa * dst[2][...] + wb * other[2][...]

    g1, g4, g16, spare = res[0:3], res[3:6], res[6:9], res[9:12]
    for src, dst in zip(g16, spare):
        for r in range(ATTN_SUB * ATTN_SUB):
            r1, r2 = divmod(r, ATTN_SUB)
            dst[pl.ds(r1 * quarter + r2, Q, stride=ATTN_SUB), :] = src[r * Q:(r + 1) * Q, :]
    merge(g4, spare)
    for src, dst in zip(g4, spare):
        for r1 in range(ATTN_SUB):
            dst[pl.ds(r1, quarter, stride=ATTN_SUB), :] = src[r1 * quarter:(r1 + 1) * quarter, :]
    merge(g1, spare)
    o_ref[...] = (g1[2][...] / g1[1][...]).astype(o_ref.dtype)


def _attn(bqkv, rel_bias):
    _, Bsz, S, _ = bqkv.shape
    ng = len(B_GROUPS)
    pairs = B_SLOTS // 2
    assert [d for _, d in B_GROUPS] == [1, ATTN_SUB, ATTN_SUB * ATTN_SUB]
    assert S % (ATTN_SUB * ATTN_SUB * B_BLOCK) == 0 and (S // B_BLOCK) % ATTN_UNROLL == 0
    idx = jnp.asarray(_bias_indices())
    in_specs, args = [], []
    for g in range(ng):
        for a in range(3):
            in_specs.append(pl.BlockSpec((None, None, S, LANES), lambda b, p, a=a, g=g: (a, b, 0, g * pairs + p)))
            args.append(bqkv)
    in_specs.append(pl.BlockSpec(idx.shape, lambda b, p: (0, 0, 0)))
    in_specs.append(pl.BlockSpec(memory_space=pltpu.SMEM))
    big = pltpu.VMEM((S, LANES), F32)
    pad = pltpu.VMEM((S + B_BLOCK, LANES), F32)
    return pl.pallas_call(
        functools.partial(_attn_kernel, seq=S),
        grid=(Bsz, pairs),
        in_specs=in_specs,
        out_specs=pl.BlockSpec((None, S, LANES), lambda b, p: (b, 0, p)),
        out_shape=jax.ShapeDtypeStruct((Bsz, S, B_OUT), BF16),
        scratch_shapes=[pltpu.VMEM((ng, 2, 2, B_BLOCK, 2 * B_BLOCK), F32), big, big, pad, pad, pad] + [big] * 12,
        compiler_params=_cparams("arbitrary", "arbitrary"),
        name="dilated_attn",
    )(*args, idx, rel_bias.astype(F32))


CONV_HALO = 32
CONV_ROWS = 128
CONV_LN_ROWS = 32


def _conv_kernel(a_ref, g_ref, w_ref, cb_ref, lg_ref, lbias_ref, o_ref, u_ref, y_ref, *, ts):
    @pl.when(pl.program_id(1) == 0)
    def _():
        u_ref[0:CONV_HALO, :] = jnp.zeros((CONV_HALO, C_WIDTH), F32)

    u_ref[CONV_HALO:CONV_HALO + ts, :] = a_ref[...].astype(F32) * jax.nn.sigmoid(g_ref[...].astype(F32))
    off = CONV_HALO - (C_KERNEL - 1)
    nwin = CONV_ROWS + CONV_HALO

    def rows(c, carry):
        r0 = pl.multiple_of(c * CONV_ROWS, CONV_ROWS)
        for cbk in range(C_WIDTH // LANES):
            ln = slice(cbk * LANES, (cbk + 1) * LANES)
            win = u_ref[pl.ds(r0, nwin), ln]
            acc = jnp.zeros((CONV_ROWS, LANES), F32)
            for s in range(8):
                taps = [j for j in range(C_KERNEL) if (off + j) % 8 == s]
                sh = win if s == 0 else pltpu.roll(win, nwin - s, 0)
                for j in taps:
                    a0 = off + j - s
                    acc = acc + w_ref[j:j + 1, ln] * sh[a0:a0 + CONV_ROWS, :]
            y_ref[:, ln] = acc + cb_ref[:, ln]
        for k in range(CONV_ROWS // CONV_LN_ROWS):
            y = y_ref[k * CONV_LN_ROWS:(k + 1) * CONV_LN_ROWS, :]
            mu = jnp.mean(y, axis=-1, keepdims=True)
            yc = y - mu
            var = jnp.mean(yc * yc, axis=-1, keepdims=True)
            o_ref[pl.ds(r0 + k * CONV_LN_ROWS, CONV_LN_ROWS), :] = _silu(
                yc * lax.rsqrt(var + EPS) * lg_ref[...] + lbias_ref[...]).astype(o_ref.dtype)
        return carry

    lax.fori_loop(0, ts // CONV_ROWS, rows, 0)
    u_ref[0:CONV_HALO, :] = u_ref[ts:ts + CONV_HALO, :]


def _conv(ca, cg, conv_w, conv_b, ln_g, ln_b):
    (arr_a, ia), (arr_g, ig) = ca, cg
    _, Bsz, S, C = arr_a.shape
    ts = min(S, 512)
    wpad = jnp.zeros((CONV_HALO, C), F32).at[:C_KERNEL].set(conv_w)
    vec = lambda v: v.reshape(1, C)
    vspec = pl.BlockSpec((1, C), lambda b, i: (0, 0))
    return pl.pallas_call(
        functools.partial(_conv_kernel, ts=ts),
        grid=(Bsz, S // ts),
        in_specs=[pl.BlockSpec((None, None, ts, C), lambda b, i: (ia, b, i, 0)),
                  pl.BlockSpec((None, None, ts, C), lambda b, i: (ig, b, i, 0)),
                  pl.BlockSpec((CONV_HALO, C), lambda b, i: (0, 0)),
                  vspec, vspec, vspec],
        out_specs=pl.BlockSpec((None, ts, C), lambda b, i: (b, i, 0)),
        out_shape=jax.ShapeDtypeStruct((Bsz, S, C), BF16),
        scratch_shapes=[pltpu.VMEM((CONV_HALO + ts, C), F32), pltpu.VMEM((CONV_ROWS, C), F32)],
        compiler_params=_cparams("arbitrary", "arbitrary"),
        name="conv_module",
    )(arr_a, arr_g, wpad, vec(conv_b), vec(ln_g), vec(ln_b))


def _mix_out_kernel(x_ref, ya_ref, yb_ref, yc_ref, g0_ref, g1_ref, g2_ref, wa_ref, wb_ref, wc_ref, wo_ref,
                    nw_ref, mod_ref, o_ref, acc_ref, *, gate_row):
    j = pl.program_id(2)

    @pl.when(j == 0)
    def _():
        acc_ref[...] = jnp.zeros_like(acc_ref)

    m = (g0_ref[...].astype(F32) * _dot(ya_ref[...], wa_ref[...])
         + g1_ref[...].astype(F32) * _dot(yb_ref[...], wb_ref[...])
         + g2_ref[...].astype(F32) * _dot(yc_ref[...], wc_ref[...]))
    acc_ref[...] += _dot(m.astype(BF16), wo_ref[...])

    @pl.when(j == pl.num_programs(2) - 1)
    def _():
        o_ref[...] = x_ref[...] + mod_ref[gate_row:gate_row + 1, :] * _rms(acc_ref[...], nw_ref[...])


def _mix_out(x, ya, yb, yc, gates, wa, wb, wc, wo, nw, mod, layer, gate_row):
    Bsz, S, D = x.shape
    tm = min(S, 512)
    tn = 512
    nj = D // tn
    tok = lambda width: pl.BlockSpec((None, tm, width), lambda b, i, j: (b, i, 0))
    gspec = lambda k: pl.BlockSpec((None, tm, tn), lambda b, i, j, k=k: (b, i, k * nj + j))
    wspec = lambda K: pl.BlockSpec((None, K, tn), lambda b, i, j: (layer, 0, j))
    return pl.pallas_call(
        functools.partial(_mix_out_kernel, gate_row=gate_row),
        grid=(Bsz, S // tm, nj),
        in_specs=[tok(D), tok(ya.shape[-1]), tok(yb.shape[-1]), tok(yc.shape[-1]),
                  gspec(0), gspec(1), gspec(2),
                  wspec(wa.shape[1]), wspec(wb.shape[1]), wspec(wc.shape[1]),
                  pl.BlockSpec((None, tn, D), lambda b, i, j: (layer, j, 0)),
                  pl.BlockSpec((1, D), lambda b, i, j: (0, 0)),
                  pl.BlockSpec((None, 6, D), lambda b, i, j: (b, 0, 0))],
        out_specs=tok(D),
        out_shape=jax.ShapeDtypeStruct((Bsz, S, D), F32),
        scratch_shapes=[pltpu.VMEM((tm, D), F32)],
        compiler_params=_cparams("arbitrary", "arbitrary", "arbitrary"),
        name="mix_out",
    )(x, ya, yb, yc, gates, gates, gates, wa, wb, wc, wo, nw.reshape(1, D), mod)


def _mlp_kernel(x_ref, wu_ref, wd_ref, npre_ref, npost_ref, mod_ref, o_ref, h_ref, acc_ref, *, sh_row, sc_row, gate_row):
    j = pl.program_id(2)

    @pl.when(j == 0)
    def _():
        y = _rms(x_ref[...], npre_ref[...])
        h_ref[...] = (y * (1.0 + mod_ref[sc_row:sc_row + 1, :]) + mod_ref[sh_row:sh_row + 1, :]).astype(h_ref.dtype)
        acc_ref[...] = jnp.zeros_like(acc_ref)

    u = jnp.maximum(_dot(h_ref[...], wu_ref[...]), 0.0)
    acc_ref[...] += _dot((u * u).astype(BF16), wd_ref[...])

    @pl.when(j == pl.num_programs(2) - 1)
    def _():
        o_ref[...] = x_ref[...] + mod_ref[gate_row:gate_row + 1, :] * _rms(acc_ref[...], npost_ref[...])


def _mlp(x, wu, wd, npre, npost, mod, layer):
    Bsz, S, D = x.shape
    FF = wu.shape[-1]
    tm = min(S, 512)
    tf = min(FF, 1024)
    tok = pl.BlockSpec((None, tm, D), lambda b, i, j: (b, i, 0))
    vspec = pl.BlockSpec((1, D), lambda b, i, j: (0, 0))
    return pl.pallas_call(
        functools.partial(_mlp_kernel, sh_row=3, sc_row=4, gate_row=5),
        grid=(Bsz, S // tm, FF // tf),
        in_specs=[tok,
                  pl.BlockSpec((None, D, tf), lambda b, i, j: (layer, 0, j)),
                  pl.BlockSpec((None, tf, D), lambda b, i, j: (layer, j, 0)),
                  vspec, vspec,
                  pl.BlockSpec((None, 6, D), lambda b, i, j: (b, 0, 0))],
        out_specs=tok,
        out_shape=jax.ShapeDtypeStruct((Bsz, S, D), F32),
        scratch_shapes=[pltpu.VMEM((tm, D), BF16), pltpu.VMEM((tm, D), F32)],
        compiler_params=_cparams("arbitrary", "arbitrary", "arbitrary"),
        name="mlp",
    )(x, wu, wd, npre.reshape(1, D), npost.reshape(1, D), mod)


def kernel(x, c, rel_bias, hgrn_lb_logits, w_ada, b_ada, mix_norm_pre, mix_norm_post, w_in, w_gate, b_gate,
           hgrn_norm_w, conv_w, conv_b, conv_ln_g, conv_ln_b, w_a_out, w_b_out, w_c_out, w_o, mlp_norm_pre,
           mlp_norm_post, w_up, w_down):
    Bsz, S, D = x.shape
    T = Bsz * S
    mod = _ada(c, w_ada, b_ada)
    wa, wb, wc, wo = _cast_bf16(w_a_out), _cast_bf16(w_b_out), _cast_bf16(w_c_out), _cast_bf16(w_o)
    wu, wd = _cast_bf16(w_up), _cast_bf16(w_down)
    lb_logits = hgrn_lb_logits.astype(F32)
    for l in range(DEPTH):
        h = _prenorm(x, mix_norm_pre[l], mod[l], 0, 1).reshape(T, D)
        a_proj = _proj(h, w_in, l, nb=1, bw=1024, col0=0, nsteps=4)
        bc = _proj(h, w_in, l, nb=3, bw=256, col0=4 * A_WIDTH, nsteps=5, stacked_out=True)
        gates = _proj(h, w_gate, l, nb=1, bw=1024, col0=0, nsteps=N_BRANCH * D // 1024, bias=b_gate)
        bc = bc.reshape(5, Bsz, S, C_WIDTH)
        ya = _hgrn(a_proj.reshape(Bsz, S, 4 * A_WIDTH), lb_logits, hgrn_norm_w[l], l)
        yb = _attn(bc, rel_bias)
        yc = _conv((bc, 3), (bc, 4), conv_w[l], conv_b[l], conv_ln_g[l], conv_ln_b[l])
        x = _mix_out(x, ya, yb, yc, gates.reshape(Bsz, S, N_BRANCH * D), wa, wb, wc, wo,
                     mix_norm_post[l], mod[l], l, 2)
        x = _mlp(x, wu, wd, mlp_norm_pre[l], mlp_norm_post[l], mod[l], l)
    return x
```

```python
import functools
import math

import numpy as np
import jax
import jax.numpy as jnp
from jax import lax
from jax.experimental import pallas as pl
from jax.experimental.pallas import tpu as pltpu

F32 = jnp.float32
BF16 = jnp.bfloat16

D_MODEL = 2048
DEPTH = 2
A_HEADS = 8
A_DIM = 128
A_WIDTH = A_HEADS * A_DIM
B_GROUPS = ((128, 1), (512, 4), (2048, 16))
B_SLOTS = 4
B_HEAD_DIM = 64
B_WIDTH = B_SLOTS * len(B_GROUPS) * B_HEAD_DIM
B_OUT = B_SLOTS * B_HEAD_DIM
B_BLOCK = 128
C_WIDTH = 768
C_KERNEL = 31
REL_BUCKETS = 32
REL_MAX_DIST = 2048
D_FF = 4 * D_MODEL
N_BRANCH = 3
IN_WIDTH = 4 * A_WIDTH + 3 * B_WIDTH + 2 * C_WIDTH
EPS = 1e-6
MASK_VALUE = -1e30
TINY = 1e-30
LOG2E = math.log2(math.e)

LANES = 128
MXU_WIDTH = 256
VMEM_LIMIT_BYTES = 60 * 1024 * 1024

HGRN_CHUNK = 128
HGRN_LEVELS = 7


def _cparams(*sem):
    return pltpu.CompilerParams(dimension_semantics=sem, vmem_limit_bytes=VMEM_LIMIT_BYTES)


def _dot(a, b):
    return jnp.dot(a, b, preferred_element_type=F32)


def _dot_nt(a, b):
    return lax.dot_general(a, b, (((1,), (1,)), ((), ())), preferred_element_type=F32)


def _dot_tn(a, b):
    return lax.dot_general(a, b, (((0,), (0,)), ((), ())), preferred_element_type=F32)


def _sigmoid(x):
    return 1.0 / (1.0 + jnp.exp2(x * (-LOG2E)))


def _silu(x):
    return x * _sigmoid(x)


def _rms(x, w):
    return x * lax.rsqrt(jnp.mean(x * x, axis=-1, keepdims=True) + EPS) * w


def _cast_kernel(x_ref, o_ref):
    o_ref[...] = x_ref[...].astype(o_ref.dtype)


def _cast_bf16(w):
    L, R, C = w.shape
    tr = next(t for t in (1024, 512, 256, 128, 64, 32, 16) if R % t == 0 and t * C * 4 <= 4 * 1024 * 1024)
    return pl.pallas_call(
        _cast_kernel,
        grid=(L, R // tr),
        in_specs=[pl.BlockSpec((None, tr, C), lambda l, i: (l, i, 0))],
        out_specs=pl.BlockSpec((None, tr, C), lambda l, i: (l, i, 0)),
        out_shape=jax.ShapeDtypeStruct(w.shape, BF16),
        compiler_params=_cparams("arbitrary", "arbitrary"),
        name="cast_bf16",
    )(w)


def _ada_kernel(c_ref, w_ref, b_ref, o_ref):
    ca = _silu(c_ref[...]).astype(BF16)
    o_ref[...] = _dot(ca, w_ref[...].astype(BF16)) + b_ref[...]


def _ada(c, w_ada, b_ada):
    Bsz, D = c.shape
    L, _, N = w_ada.shape
    rows = 16
    cp = jnp.zeros((rows, D), F32).at[:Bsz].set(c)
    tn = 1024
    out = pl.pallas_call(
        _ada_kernel,
        grid=(L, N // tn),
        in_specs=[pl.BlockSpec((rows, D), lambda l, j: (0, 0)),
                  pl.BlockSpec((None, D, tn), lambda l, j: (l, 0, j)),
                  pl.BlockSpec((None, 1, tn), lambda l, j: (l, 0, j))],
        out_specs=pl.BlockSpec((None, rows, tn), lambda l, j: (l, 0, j)),
        out_shape=jax.ShapeDtypeStruct((L, rows, N), F32),
        compiler_params=_cparams("arbitrary", "arbitrary"),
        name="ada_mod",
    )(cp, w_ada, b_ada.reshape(L, 1, N))
    return out[:, :Bsz].reshape(L, Bsz, 6, D)


def _prenorm_kernel(x_ref, nw_ref, mod_ref, o_ref, *, sh_row, sc_row):
    y = _rms(x_ref[...], nw_ref[...])
    o_ref[...] = (y * (1.0 + mod_ref[sc_row:sc_row + 1, :]) + mod_ref[sh_row:sh_row + 1, :]).astype(o_ref.dtype)


def _prenorm(x, nw, mod, sh_row, sc_row):
    Bsz, S, D = x.shape
    ts = min(S, 512)
    return pl.pallas_call(
        functools.partial(_prenorm_kernel, sh_row=sh_row, sc_row=sc_row),
        grid=(Bsz, S // ts),
        in_specs=[pl.BlockSpec((None, ts, D), lambda b, i: (b, i, 0)),
                  pl.BlockSpec((1, D), lambda b, i: (0, 0)),
                  pl.BlockSpec((None, 6, D), lambda b, i: (b, 0, 0))],
        out_specs=pl.BlockSpec((None, ts, D), lambda b, i: (b, i, 0)),
        out_shape=jax.ShapeDtypeStruct((Bsz, S, D), BF16),
        compiler_params=_cparams("arbitrary", "arbitrary"),
        name="prenorm",
    )(x, nw.reshape(1, D), mod)


def _proj_kernel(*refs, nb, bw, gate):
    h_ref = refs[0]
    w_refs = refs[1:1 + nb]
    b_ref = refs[1 + nb] if gate else None
    o_ref = refs[-2]
    wb_ref = refs[-1]

    @pl.when(pl.program_id(1) == 0)
    def _():
        for k in range(nb):
            wb_ref[k] = w_refs[k][...].astype(BF16)

    h = h_ref[...]
    for k in range(nb):
        acc = _dot(h, wb_ref[k])
        if gate:
            acc = _sigmoid(acc + b_ref[:, k * bw:(k + 1) * bw])
        o_ref[:, k * bw:(k + 1) * bw] = acc.astype(o_ref.dtype)


def _proj(h2, w, layer, *, nb, bw, col0, nsteps, bias=None, stacked_out=False):
    T, D = h2.shape
    tm = min(T, 2048)
    tn = nb * bw
    assert col0 % bw == 0
    blk0 = col0 // bw
    in_specs = [pl.BlockSpec((tm, D), lambda j, i: (i, 0))]
    args = [h2]
    for k in range(nb):
        in_specs.append(pl.BlockSpec((None, D, bw), lambda j, i, k=k: (layer, 0, blk0 + j * nb + k)))
        args.append(w)
    if bias is not None:
        in_specs.append(pl.BlockSpec((None, 1, tn), lambda j, i: (layer, 0, j)))
        args.append(bias.reshape(bias.shape[0], 1, bias.shape[1]))
    if stacked_out:
        out_shape = jax.ShapeDtypeStruct((nsteps, T, tn), BF16)
        out_spec = pl.BlockSpec((None, tm, tn), lambda j, i: (j, i, 0))
    else:
        out_shape = jax.ShapeDtypeStruct((T, nsteps * tn), BF16)
        out_spec = pl.BlockSpec((tm, tn), lambda j, i: (i, j))
    return pl.pallas_call(
        functools.partial(_proj_kernel, nb=nb, bw=bw, gate=bias is not None),
        grid=(nsteps, T // tm),
        in_specs=in_specs,
        out_specs=out_spec,
        out_shape=out_shape,
        scratch_shapes=[pltpu.VMEM((nb, D, bw), BF16)],
        compiler_params=_cparams("arbitrary", "arbitrary"),
        name="in_proj",
    )(*args)


def _hgrn_constants():
    C = HGRN_CHUNK
    t = np.arange(C)[:, None]
    u = np.arange(C)[None, :]
    mats, masks = [], []
    for li in range(HGRN_LEVELS):
        w = 1 << li
        same = (u // (2 * w)) == (t // (2 * w))
        mid = (t // (2 * w)) * (2 * w) + w - 1
        upper = (t // w) % 2 == 1
        mats.append(np.where(upper, same & (u > mid) & (u <= t), same & (u > t) & (u <= mid)))
        masks.append(same & upper & ((u // w) % 2 == 0))
    mats.append(u <= t)
    mats.append(u > t)
    m = np.concatenate(mats, axis=0).astype(np.float32)
    return np.concatenate([m, m], axis=1), np.stack(masks, axis=0).astype(np.float32)


def _hgrn_kernel(q_ref, f_ref, i_ref, g_ref, lbl_ref, nw_ref, mst_ref, msk_ref, o_ref, st_ref, *, layer, hp, nchunk):
    C = HGRN_CHUNK
    NL = HGRN_LEVELS
    logits = lbl_ref[...]
    e = jnp.exp(logits - jnp.max(logits, axis=0, keepdims=True))
    sm = e / jnp.sum(e, axis=0, keepdims=True)
    lower = jnp.sum(sm[:layer + 1], axis=0, keepdims=True) - sm[0:1]
    lb = jnp.maximum(lower, 0.0)
    lb_tiny = lb + TINY
    one_m = 1.0 - lb
    nw = nw_ref[...]
    st_ref[...] = jnp.zeros_like(st_ref)

    def chunk(c, carry):
        r0 = pl.multiple_of(c * C, C)
        z = f_ref[pl.ds(r0, C), :].astype(F32)
        qp = q_ref[pl.ds(r0, C), :].astype(F32)
        v = i_ref[pl.ds(r0, C), :]
        og = g_ref[pl.ds(r0, C), :].astype(F32)
        gs = one_m * _sigmoid(z)
        lf2 = jnp.log2(lb_tiny + gs)
        kk = one_m - gs
        q = _silu(qp)
        lf_hi = lf2.astype(BF16)
        lf_lo = (lf2 - lf_hi.astype(F32)).astype(BF16)
        ex = jnp.exp2(_dot(mst_ref[...], jnp.concatenate([lf_hi, lf_lo], axis=0)))
        for hh in range(hp):
            sl = slice(hh * LANES, (hh + 1) * LANES)
            q_h, k_h, v_h = q[:, sl], kk[:, sl], v[:, sl]
            att = None
            for li in range(NL):
                e_l = ex[li * C:(li + 1) * C, sl]
                s_l = msk_ref[li] * _dot_nt((q_h * e_l).astype(BF16), (k_h * e_l).astype(BF16))
                att = s_l if att is None else att + s_l
            diag = jnp.sum(q_h * k_h, axis=-1, keepdims=True)
            o = _dot(att.astype(BF16), v_h) + diag * v_h.astype(F32)
            st = st_ref[hh]
            q_in = (q_h * ex[NL * C:(NL + 1) * C, sl]).astype(BF16)
            o = o + _dot_nt(q_in, st.astype(BF16))
            k_in = (k_h * ex[(NL + 1) * C:(NL + 2) * C, sl]).astype(BF16)
            d_row = ex[(NL + 1) * C - 1:(NL + 1) * C, sl]
            st_ref[hh] = st * d_row + _dot_tn(v_h, k_in)
            y = _rms(o, nw) * _silu(og[:, sl])
            o_ref[pl.ds(r0, C), sl] = y.astype(o_ref.dtype)
        return carry

    lax.fori_loop(0, nchunk, chunk, 0, unroll=2)


def _hgrn(a_proj, lb_logits, norm_w, layer, hp=4):
    Bsz, S, _ = a_proj.shape
    mst, msk = _hgrn_constants()
    bw = hp * A_DIM
    nhb = A_WIDTH // bw
    specs = [pl.BlockSpec((None, S, bw), lambda b, h, k=k: (b, 0, k * nhb + h)) for k in range(4)]
    return pl.pallas_call(
        functools.partial(_hgrn_kernel, layer=layer, hp=hp, nchunk=S // HGRN_CHUNK),
        grid=(Bsz, nhb),
        in_specs=specs + [
            pl.BlockSpec((DEPTH, bw), lambda b, h: (0, h)),
            pl.BlockSpec((1, A_DIM), lambda b, h: (0, 0)),
            pl.BlockSpec(mst.shape, lambda b, h: (0, 0)),
            pl.BlockSpec(msk.shape, lambda b, h: (0, 0, 0)),
        ],
        out_specs=pl.BlockSpec((None, S, bw), lambda b, h: (b, 0, h)),
        out_shape=jax.ShapeDtypeStruct((Bsz, S, A_WIDTH), BF16),
        scratch_shapes=[pltpu.VMEM((hp, A_DIM, A_DIM), F32)],
        compiler_params=_cparams("arbitrary", "arbitrary"),
        name="hgrn2",
    )(a_proj, a_proj, a_proj, a_proj, lb_logits, norm_w.reshape(1, A_DIM),
      jnp.asarray(mst, BF16), jnp.asarray(msk, F32))


def _bias_indices():
    Q = B_BLOCK
    steps = np.arange(Q)[:, None] + Q - np.arange(2 * Q)[None, :]
    valid = (steps >= 0) & (steps <= Q)
    exact = REL_BUCKETS // 2
    idx = []
    for _, dil in B_GROUPS:
        dist = np.clip(steps, 0, Q) * dil
        d = np.maximum(dist, 1).astype(np.float32)
        large = exact + (np.log(d / np.float32(exact)) / np.float32(math.log(REL_MAX_DIST / exact))
                         * np.float32(REL_BUCKETS - exact)).astype(np.int32)
        idx.append(np.where(dist < exact, dist, np.clip(large, exact, REL_BUCKETS - 1)))
    return np.stack(idx, axis=0).astype(np.int32)


def _build_bias_tables(idx_ref, rel_ref, tab_ref, pair):
    Q = B_BLOCK
    qi = lax.broadcasted_iota(jnp.int32, (Q, 2 * Q), 0)
    ki = lax.broadcasted_iota(jnp.int32, (Q, 2 * Q), 1)
    steps = qi + Q - ki
    valid = (steps >= 0) & (steps <= Q)
    valid_first = valid & (ki >= Q)
    for g in range(len(B_GROUPS)):
        idx = idx_ref[g]
        for s in range(2):
            head = g * B_SLOTS + 2 * pair + s
            bias = jnp.zeros((Q, 2 * Q), F32)
            for b in range(REL_BUCKETS):
                bias = jnp.where(idx == b, rel_ref[b, head], bias)
            bias = bias * LOG2E
            tab_ref[g, s, 0] = jnp.where(valid_first, bias, MASK_VALUE)
            tab_ref[g, s, 1] = jnp.where(valid, bias, MASK_VALUE)


ATTN_UNROLL = 4
ATTN_SUB = 4


def _attn_kernel(*refs, seq):
    ng = len(B_GROUPS)
    qkv = refs[:3 * ng]
    idx_ref, rel_ref, o_ref = refs[3 * ng:3 * ng + 3]
    tab_ref, tmp_ref, tmp2_ref, qd_ref, kd_ref, vd_ref = refs[3 * ng + 3:3 * ng + 9]
    res = refs[3 * ng + 9:]
    Q = B_BLOCK
    lane = lax.broadcasted_iota(jnp.int32, (Q, LANES), 1)
    low = lane < B_HEAD_DIM
    _build_bias_tables(idx_ref, rel_ref, tab_ref, pl.program_id(1))
    kd_ref[0:Q, :] = jnp.zeros((Q, LANES), F32)
    vd_ref[0:Q, :] = jnp.zeros((Q, LANES), F32)
    quarter = seq // ATTN_SUB

    for g, (_, dil) in enumerate(B_GROUPS):
        nblk = seq // dil // Q
        for src, dst in zip(qkv[3 * g:3 * g + 3], (qd_ref, kd_ref, vd_ref)):
            if dil == 1:
                dst[Q:Q + seq, :] = src[...].astype(F32)
                continue
            tmp_ref[...] = src[...].astype(F32)
            mid = dst if dil == ATTN_SUB else tmp2_ref
            off = Q if dil == ATTN_SUB else 0
            for r1 in range(ATTN_SUB):
                mid[off + r1 * quarter:off + (r1 + 1) * quarter, :] = tmp_ref[pl.ds(r1, quarter, stride=ATTN_SUB), :]
            if dil != ATTN_SUB:
                for r in range(ATTN_SUB * ATTN_SUB):
                    r1, r2 = divmod(r, ATTN_SUB)
                    dst[Q + r * Q:Q + (r + 1) * Q, :] = tmp2_ref[pl.ds(r1 * quarter + r2, Q, stride=ATTN_SUB), :]
        mo, lo, ao = res[3 * g:3 * g + 3]

        def units(it, carry, g=g, nblk=nblk, mo=mo, lo=lo, ao=ao):
            for uu in range(ATTN_UNROLL):
                u = it * ATTN_UNROLL + uu
                base = pl.multiple_of(u * Q, Q)
                var = jnp.where(u % nblk == 0, 0, 1)
                qb = qd_ref[pl.ds(base + Q, Q), :] * (B_HEAD_DIM ** -0.5 * LOG2E)
                kb = kd_ref[pl.ds(base, 2 * Q), :].astype(BF16)
                vb = vd_ref[pl.ds(base, 2 * Q), :].astype(BF16)
                ms, ls, pv = [], [], []
                for s in range(2):
                    qs = jnp.where(low if s == 0 else ~low, qb, 0.0).astype(BF16)
                    sc = _dot_nt(qs, kb) + tab_ref[g, s, var]
                    m = jnp.max(sc, axis=-1, keepdims=True)
                    p = jnp.exp2(sc - m)
                    ms.append(m)
                    ls.append(jnp.sum(p, axis=-1, keepdims=True))
                    pv.append(_dot(p.astype(BF16), vb))
                mo[pl.ds(base, Q), :] = jnp.where(low, ms[0], ms[1])
                lo[pl.ds(base, Q), :] = jnp.where(low, ls[0], ls[1])
                ao[pl.ds(base, Q), :] = jnp.where(low, pv[0], pv[1])
            return carry

        lax.fori_loop(0, seq // Q // ATTN_UNROLL, units, 0)

    def merge(dst, other):
        m_a, m_b = dst[0][...], other[0][...]
        mx = jnp.maximum(m_a, m_b)
        wa, wb = jnp.exp2(m_a - mx), jnp.exp2(m_b - mx)
        dst[0][...] = mx
        dst[1][...] = wa * dst[1][...] + wb * other[1][...]
        dst[2][...] = wa * dst[2][...] + wb * other[2][...]

    g1, g4, g16, spare = res[0:3], res[3:6], res[6:9], res[9:12]
    for src, dst in zip(g16, spare):
        for r in range(ATTN_SUB * ATTN_SUB):
            r1, r2 = divmod(r, ATTN_SUB)
            dst[pl.ds(r1 * quarter + r2, Q, stride=ATTN_SUB), :] = src[r * Q:(r + 1) * Q, :]
    merge(g4, spare)
    for src, dst in zip(g4, spare):
        for r1 in range(ATTN_SUB):
            dst[pl.ds(r1, quarter, stride=ATTN_SUB), :] = src[r1 * quarter:(r1 + 1) * quarter, :]
    merge(g1, spare)
    o_ref[...] = (g1[2][...] / g1[1][...]).astype(o_ref.dtype)


def _attn(bqkv, rel_bias):
    _, Bsz, S, _ = bqkv.shape
    ng = len(B_GROUPS)
    pairs = B_SLOTS // 2
    assert [d for _, d in B_GROUPS] == [1, ATTN_SUB, ATTN_SUB * ATTN_SUB]
    assert S % (ATTN_SUB * ATTN_SUB * B_BLOCK) == 0 and (S // B_BLOCK) % ATTN_UNROLL == 0
    idx = jnp.asarray(_bias_indices())
    in_specs, args = [], []
    for g in range(ng):
        for a in range(3):
            in_specs.append(pl.BlockSpec((None, None, S, LANES), lambda b, p, a=a, g=g: (a, b, 0, g * pairs + p)))
            args.append(bqkv)
    in_specs.append(pl.BlockSpec(idx.shape, lambda b, p: (0, 0, 0)))
    in_specs.append(pl.BlockSpec(memory_space=pltpu.SMEM))
    big = pltpu.VMEM((S, LANES), F32)
    pad = pltpu.VMEM((S + B_BLOCK, LANES), F32)
    return pl.pallas_call(
        functools.partial(_attn_kernel, seq=S),
        grid=(Bsz, pairs),
        in_specs=in_specs,
        out_specs=pl.BlockSpec((None, S, LANES), lambda b, p: (b, 0, p)),
        out_shape=jax.ShapeDtypeStruct((Bsz, S, B_OUT), BF16),
        scratch_shapes=[pltpu.VMEM((ng, 2, 2, B_BLOCK, 2 * B_BLOCK), F32), big, big, pad, pad, pad] + [big] * 12,
        compiler_params=_cparams("arbitrary", "arbitrary"),
        name="dilated_attn",
    )(*args, idx, rel_bias.astype(F32))


CONV_HALO = 32
CONV_ROWS = 128
CONV_LN_ROWS = 32


def _conv_kernel(a_ref, g_ref, w_ref, cb_ref, lg_ref, lbias_ref, o_ref, u_ref, y_ref, *, ts):
    @pl.when(pl.program_id(1) == 0)
    def _():
        u_ref[0:CONV_HALO, :] = jnp.zeros((CONV_HALO, C_WIDTH), F32)

    u_ref[CONV_HALO:CONV_HALO + ts, :] = a_ref[...].astype(F32) * _sigmoid(g_ref[...].astype(F32))
    off = CONV_HALO - (C_KERNEL - 1)
    nwin = CONV_ROWS + CONV_HALO

    def rows(c, carry):
        r0 = pl.multiple_of(c * CONV_ROWS, CONV_ROWS)
        for cbk in range(C_WIDTH // LANES):
            ln = slice(cbk * LANES, (cbk + 1) * LANES)
            win = u_ref[pl.ds(r0, nwin), ln]
            acc = jnp.zeros((CONV_ROWS, LANES), F32)
            for s in range(8):
                taps = [j for j in range(C_KERNEL) if (off + j) % 8 == s]
                sh = win if s == 0 else pltpu.roll(win, nwin - s, 0)
                for j in taps:
                    a0 = off + j - s
                    acc = acc + w_ref[j:j + 1, ln] * sh[a0:a0 + CONV_ROWS, :]
            y_ref[:, ln] = acc + cb_ref[:, ln]
        for k in range(CONV_ROWS // CONV_LN_ROWS):
            y = y_ref[k * CONV_LN_ROWS:(k + 1) * CONV_LN_ROWS, :]
            mu = jnp.mean(y, axis=-1, keepdims=True)
            yc = y - mu
            var = jnp.mean(yc * yc, axis=-1, keepdims=True)
            o_ref[pl.ds(r0 + k * CONV_LN_ROWS, CONV_LN_ROWS), :] = _silu(
                yc * lax.rsqrt(var + EPS) * lg_ref[...] + lbias_ref[...]).astype(o_ref.dtype)
        return carry

    lax.fori_loop(0, ts // CONV_ROWS, rows, 0)
    u_ref[0:CONV_HALO, :] = u_ref[ts:ts + CONV_HALO, :]


def _conv(ca, cg, conv_w, conv_b, ln_g, ln_b):
    (arr_a, ia), (arr_g, ig) = ca, cg
    _, Bsz, S, C = arr_a.shape
    ts = min(S, 512)
    wpad = jnp.zeros((CONV_HALO, C), F32).at[:C_KERNEL].set(conv_w)
    vec = lambda v: v.reshape(1, C)
    vspec = pl.BlockSpec((1, C), lambda b, i: (0, 0))
    return pl.pallas_call(
        functools.partial(_conv_kernel, ts=ts),
        grid=(Bsz, S // ts),
        in_specs=[pl.BlockSpec((None, None, ts, C), lambda b, i: (ia, b, i, 0)),
                  pl.BlockSpec((None, None, ts, C), lambda b, i: (ig, b, i, 0)),
                  pl.BlockSpec((CONV_HALO, C), lambda b, i: (0, 0)),
                  vspec, vspec, vspec],
        out_specs=pl.BlockSpec((None, ts, C), lambda b, i: (b, i, 0)),
        out_shape=jax.ShapeDtypeStruct((Bsz, S, C), BF16),
        scratch_shapes=[pltpu.VMEM((CONV_HALO + ts, C), F32), pltpu.VMEM((CONV_ROWS, C), F32)],
        compiler_params=_cparams("arbitrary", "arbitrary"),
        name="conv_module",
    )(arr_a, arr_g, wpad, vec(conv_b), vec(ln_g), vec(ln_b))


def _mix_out_kernel(ya_ref, yb_ref, yc_ref, g0_ref, g1_ref, g2_ref, wa_ref, wb_ref, wc_ref, wo_ref,
                    nw_ref, mod_ref, o_ref, acc_ref, *, gate_row):
    j = pl.program_id(2)

    @pl.when(j == 0)
    def _():
        acc_ref[...] = jnp.zeros_like(acc_ref)

    m = (g0_ref[...].astype(F32) * _dot(ya_ref[...], wa_ref[...])
         + g1_ref[...].astype(F32) * _dot(yb_ref[...], wb_ref[...])
         + g2_ref[...].astype(F32) * _dot(yc_ref[...], wc_ref[...]))
    acc_ref[...] += _dot(m.astype(BF16), wo_ref[...])

    @pl.when(j == pl.num_programs(2) - 1)
    def _():
        o_ref[...] = mod_ref[gate_row:gate_row + 1, :] * _rms(acc_ref[...], nw_ref[...])


def _mix_out(ya, yb, yc, gates, wa, wb, wc, wo, nw, mod, layer, gate_row):
    Bsz, S, _ = ya.shape
    D = wo.shape[-1]
    tm = min(S, 1024)
    tn = 512
    nj = D // tn
    tok = lambda width: pl.BlockSpec((None, tm, width), lambda b, i, j: (b, i, 0))
    gspec = lambda k: pl.BlockSpec((None, tm, tn), lambda b, i, j, k=k: (b, i, k * nj + j))
    wspec = lambda K: pl.BlockSpec((None, K, tn), lambda b, i, j: (layer, 0, j))
    return pl.pallas_call(
        functools.partial(_mix_out_kernel, gate_row=gate_row),
        grid=(Bsz, S // tm, nj),
        in_specs=[tok(ya.shape[-1]), tok(yb.shape[-1]), tok(yc.shape[-1]),
                  gspec(0), gspec(1), gspec(2),
                  wspec(wa.shape[1]), wspec(wb.shape[1]), wspec(wc.shape[1]),
                  pl.BlockSpec((None, tn, D), lambda b, i, j: (layer, j, 0)),
                  pl.BlockSpec((1, D), lambda b, i, j: (0, 0)),
                  pl.BlockSpec((None, 6, D), lambda b, i, j: (b, 0, 0))],
        out_specs=tok(D),
        out_shape=jax.ShapeDtypeStruct((Bsz, S, D), F32),
        scratch_shapes=[pltpu.VMEM((tm, D), F32)],
        compiler_params=_cparams("arbitrary", "arbitrary", "arbitrary"),
        name="mix_out",
    )(ya, yb, yc, gates, gates, gates, wa, wb, wc, wo, nw.reshape(1, D), mod)


def _mlp_kernel(x_ref, d_ref, wu_ref, wd_ref, npre_ref, npost_ref, mod_ref, o_ref, xn_ref, h_ref, acc_ref,
                *, sh_row, sc_row, gate_row):
    j = pl.program_id(2)

    @pl.when(j == 0)
    def _():
        xn = x_ref[...] + d_ref[...]
        xn_ref[...] = xn
        y = _rms(xn, npre_ref[...])
        h_ref[...] = (y * (1.0 + mod_ref[sc_row:sc_row + 1, :]) + mod_ref[sh_row:sh_row + 1, :]).astype(h_ref.dtype)
        acc_ref[...] = jnp.zeros_like(acc_ref)

    u = jnp.maximum(_dot(h_ref[...], wu_ref[...]), 0.0)
    acc_ref[...] += _dot((u * u).astype(BF16), wd_ref[...])

    @pl.when(j == pl.num_programs(2) - 1)
    def _():
        o_ref[...] = xn_ref[...] + mod_ref[gate_row:gate_row + 1, :] * _rms(acc_ref[...], npost_ref[...])


def _mlp(x, delta, wu, wd, npre, npost, mod, layer):
    Bsz, S, D = x.shape
    FF = wu.shape[-1]
    tm = min(S, 512)
    tf = min(FF, 1024)
    tok = pl.BlockSpec((None, tm, D), lambda b, i, j: (b, i, 0))
    vspec = pl.BlockSpec((1, D), lambda b, i, j: (0, 0))
    return pl.pallas_call(
        functools.partial(_mlp_kernel, sh_row=3, sc_row=4, gate_row=5),
        grid=(Bsz, S // tm, FF // tf),
        in_specs=[tok, tok,
                  pl.BlockSpec((None, D, tf), lambda b, i, j: (layer, 0, j)),
                  pl.BlockSpec((None, tf, D), lambda b, i, j: (layer, j, 0)),
                  vspec, vspec,
                  pl.BlockSpec((None, 6, D), lambda b, i, j: (b, 0, 0))],
        out_specs=tok,
        out_shape=jax.ShapeDtypeStruct((Bsz, S, D), F32),
        scratch_shapes=[pltpu.VMEM((tm, D), F32), pltpu.VMEM((tm, D), BF16), pltpu.VMEM((tm, D), F32)],
        compiler_params=_cparams("arbitrary", "arbitrary", "arbitrary"),
        name="mlp",
    )(x, delta, wu, wd, npre.reshape(1, D), npost.reshape(1, D), mod)


def kernel(x, c, rel_bias, hgrn_lb_logits, w_ada, b_ada, mix_norm_pre, mix_norm_post, w_in, w_gate, b_gate,
           hgrn_norm_w, conv_w, conv_b, conv_ln_g, conv_ln_b, w_a_out, w_b_out, w_c_out, w_o, mlp_norm_pre,
           mlp_norm_post, w_up, w_down):
    Bsz, S, D = x.shape
    T = Bsz * S
    mod = _ada(c, w_ada, b_ada)
    wa, wb, wc, wo = _cast_bf16(w_a_out), _cast_bf16(w_b_out), _cast_bf16(w_c_out), _cast_bf16(w_o)
    wu, wd = _cast_bf16(w_up), _cast_bf16(w_down)
    lb_logits = hgrn_lb_logits.astype(F32)
    for l in range(DEPTH):
        h = _prenorm(x, mix_norm_pre[l], mod[l], 0, 1).reshape(T, D)
        a_proj = _proj(h, w_in, l, nb=1, bw=1024, col0=0, nsteps=4)
        bc = _proj(h, w_in, l, nb=3, bw=256, col0=4 * A_WIDTH, nsteps=5, stacked_out=True)
        gates = _proj(h, w_gate, l, nb=1, bw=1024, col0=0, nsteps=N_BRANCH * D // 1024, bias=b_gate)
        bc = bc.reshape(5, Bsz, S, C_WIDTH)
        ya = _hgrn(a_proj.reshape(Bsz, S, 4 * A_WIDTH), lb_logits, hgrn_norm_w[l], l)
        yb = _attn(bc, rel_bias)
        yc = _conv((bc, 3), (bc, 4), conv_w[l], conv_b[l], conv_ln_g[l], conv_ln_b[l])
        delta = _mix_out(ya, yb, yc, gates.reshape(Bsz, S, N_BRANCH * D), wa, wb, wc, wo,
                         mix_norm_post[l], mod[l], l, 2)
        x = _mlp(x, delta, wu, wd, mlp_norm_pre[l], mlp_norm_post[l], mod[l], l)
    return x
```

```python
import functools
import math

import numpy as np
import jax
import jax.numpy as jnp
from jax import lax
from jax.experimental import pallas as pl
from jax.experimental.pallas import tpu as pltpu

F32 = jnp.float32
BF16 = jnp.bfloat16

D_MODEL = 2048
DEPTH = 2
A_HEADS = 8
A_DIM = 128
A_WIDTH = A_HEADS * A_DIM
B_GROUPS = ((128, 1), (512, 4), (2048, 16))
B_SLOTS = 4
B_HEAD_DIM = 64
B_WIDTH = B_SLOTS * len(B_GROUPS) * B_HEAD_DIM
B_OUT = B_SLOTS * B_HEAD_DIM
B_BLOCK = 128
C_WIDTH = 768
C_KERNEL = 31
REL_BUCKETS = 32
REL_MAX_DIST = 2048
D_FF = 4 * D_MODEL
N_BRANCH = 3
IN_WIDTH = 4 * A_WIDTH + 3 * B_WIDTH + 2 * C_WIDTH
EPS = 1e-6
MASK_VALUE = -1e30
TINY = 1e-30
LOG2E = math.log2(math.e)

LANES = 128
MXU_WIDTH = 256
VMEM_LIMIT_BYTES = 60 * 1024 * 1024

PROJ_ROWS = 256
EPI_ROWS = 256
HGRN_CHUNK = 128
HGRN_LEVELS = 7


def _cparams(*sem):
    return pltpu.CompilerParams(dimension_semantics=sem, vmem_limit_bytes=VMEM_LIMIT_BYTES)


def _dot(a, b):
    return jnp.dot(a, b, preferred_element_type=F32)


def _dot_nt(a, b):
    return lax.dot_general(a, b, (((1,), (1,)), ((), ())), preferred_element_type=F32)


def _dot_tn(a, b):
    return lax.dot_general(a, b, (((0,), (0,)), ((), ())), preferred_element_type=F32)


def _sigmoid(x):
    return 1.0 / (1.0 + jnp.exp2(x * (-LOG2E)))


def _silu(x):
    return x * _sigmoid(x)


def _rms(x, w):
    return x * lax.rsqrt(jnp.mean(x * x, axis=-1, keepdims=True) + EPS) * w


def _cast_kernel(x_ref, o_ref):
    o_ref[...] = x_ref[...].astype(o_ref.dtype)


def _cast_bf16(w):
    L, R, C = w.shape
    tr = next(t for t in (1024, 512, 256, 128, 64, 32, 16) if R % t == 0 and t * C * 4 <= 4 * 1024 * 1024)
    return pl.pallas_call(
        _cast_kernel,
        grid=(L, R // tr),
        in_specs=[pl.BlockSpec((None, tr, C), lambda l, i: (l, i, 0))],
        out_specs=pl.BlockSpec((None, tr, C), lambda l, i: (l, i, 0)),
        out_shape=jax.ShapeDtypeStruct(w.shape, BF16),
        compiler_params=_cparams("arbitrary", "arbitrary"),
        name="cast_bf16",
    )(w)


def _ada_kernel(c_ref, w_ref, b_ref, o_ref):
    ca = _silu(c_ref[...]).astype(BF16)
    o_ref[...] = _dot(ca, w_ref[...].astype(BF16)) + b_ref[...]


def _ada(c, w_ada, b_ada):
    Bsz, D = c.shape
    L, _, N = w_ada.shape
    rows = 16
    cp = jnp.zeros((rows, D), F32).at[:Bsz].set(c)
    tn = 1024
    out = pl.pallas_call(
        _ada_kernel,
        grid=(L, N // tn),
        in_specs=[pl.BlockSpec((rows, D), lambda l, j: (0, 0)),
                  pl.BlockSpec((None, D, tn), lambda l, j: (l, 0, j)),
                  pl.BlockSpec((None, 1, tn), lambda l, j: (l, 0, j))],
        out_specs=pl.BlockSpec((None, rows, tn), lambda l, j: (l, 0, j)),
        out_shape=jax.ShapeDtypeStruct((L, rows, N), F32),
        compiler_params=_cparams("arbitrary", "arbitrary"),
        name="ada_mod",
    )(cp, w_ada, b_ada.reshape(L, 1, N))
    return out[:, :Bsz].reshape(L, Bsz, 6, D)


def _prenorm_kernel(x_ref, nw_ref, mod_ref, o_ref, *, sh_row, sc_row):
    y = _rms(x_ref[...], nw_ref[...])
    o_ref[...] = (y * (1.0 + mod_ref[sc_row:sc_row + 1, :]) + mod_ref[sh_row:sh_row + 1, :]).astype(o_ref.dtype)


def _prenorm(x, nw, mod, sh_row, sc_row):
    Bsz, S, D = x.shape
    ts = min(S, 512)
    return pl.pallas_call(
        functools.partial(_prenorm_kernel, sh_row=sh_row, sc_row=sc_row),
        grid=(Bsz, S // ts),
        in_specs=[pl.BlockSpec((None, ts, D), lambda b, i: (b, i, 0)),
                  pl.BlockSpec((1, D), lambda b, i: (0, 0)),
                  pl.BlockSpec((None, 6, D), lambda b, i: (b, 0, 0))],
        out_specs=pl.BlockSpec((None, ts, D), lambda b, i: (b, i, 0)),
        out_shape=jax.ShapeDtypeStruct((Bsz, S, D), BF16),
        compiler_params=_cparams("arbitrary", "arbitrary"),
        name="prenorm",
    )(x, nw.reshape(1, D), mod)


def _proj_kernel(*refs, nb, bw, gate):
    h_ref = refs[0]
    w_refs = refs[1:1 + nb]
    b_ref = refs[1 + nb] if gate else None
    o_ref = refs[-2]
    wb_ref = refs[-1]

    @pl.when(pl.program_id(1) == 0)
    def _():
        for k in range(nb):
            wb_ref[k] = w_refs[k][...].astype(BF16)

    tm = h_ref.shape[0]
    rb = min(tm, PROJ_ROWS)
    for r in range(tm // rb):
        h = h_ref[r * rb:(r + 1) * rb, :]
        for k in range(nb):
            acc = _dot(h, wb_ref[k])
            if gate:
                acc = _sigmoid(acc + b_ref[:, k * bw:(k + 1) * bw])
            o_ref[r * rb:(r + 1) * rb, k * bw:(k + 1) * bw] = acc.astype(o_ref.dtype)


def _proj(h2, w, layer, *, nb, bw, col0, nsteps, bias=None, stacked_out=False):
    T, D = h2.shape
    tm = min(T, 2048)
    tn = nb * bw
    assert col0 % bw == 0
    blk0 = col0 // bw
    in_specs = [pl.BlockSpec((tm, D), lambda j, i: (i, 0))]
    args = [h2]
    for k in range(nb):
        in_specs.append(pl.BlockSpec((None, D, bw), lambda j, i, k=k: (layer, 0, blk0 + j * nb + k)))
        args.append(w)
    if bias is not None:
        in_specs.append(pl.BlockSpec((None, 1, tn), lambda j, i: (layer, 0, j)))
        args.append(bias.reshape(bias.shape[0], 1, bias.shape[1]))
    if stacked_out:
        out_shape = jax.ShapeDtypeStruct((nsteps, T, tn), BF16)
        out_spec = pl.BlockSpec((None, tm, tn), lambda j, i: (j, i, 0))
    else:
        out_shape = jax.ShapeDtypeStruct((T, nsteps * tn), BF16)
        out_spec = pl.BlockSpec((tm, tn), lambda j, i: (i, j))
    return pl.pallas_call(
        functools.partial(_proj_kernel, nb=nb, bw=bw, gate=bias is not None),
        grid=(nsteps, T // tm),
        in_specs=in_specs,
        out_specs=out_spec,
        out_shape=out_shape,
        scratch_shapes=[pltpu.VMEM((nb, D, bw), BF16)],
        compiler_params=_cparams("arbitrary", "arbitrary"),
        name="in_proj",
    )(*args)


def _hgrn_constants():
    C = HGRN_CHUNK
    t = np.arange(C)[:, None]
    u = np.arange(C)[None, :]
    mats, masks = [], []
    for li in range(HGRN_LEVELS):
        w = 1 << li
        same = (u // (2 * w)) == (t // (2 * w))
        mid = (t // (2 * w)) * (2 * w) + w - 1
        upper = (t // w) % 2 == 1
        mats.append(np.where(upper, same & (u > mid) & (u <= t), same & (u > t) & (u <= mid)))
        masks.append(same & upper & ((u // w) % 2 == 0))
    mats.append(u <= t)
    mats.append(u > t)
    m = np.concatenate(mats, axis=0).astype(np.float32)
    return np.concatenate([m, m], axis=1), np.stack(masks, axis=0).astype(np.float32)


def _hgrn_kernel(q_ref, f_ref, i_ref, g_ref, lbl_ref, nw_ref, mst_ref, msk_ref, o_ref, st_ref, *, layer, hp, nchunk):
    C = HGRN_CHUNK
    NL = HGRN_LEVELS
    logits = lbl_ref[...]
    e = jnp.exp(logits - jnp.max(logits, axis=0, keepdims=True))
    sm = e / jnp.sum(e, axis=0, keepdims=True)
    lower = jnp.sum(sm[:layer + 1], axis=0, keepdims=True) - sm[0:1]
    lb = jnp.maximum(lower, 0.0)
    lb_tiny = lb + TINY
    one_m = 1.0 - lb
    nw = nw_ref[...]
    st_ref[...] = jnp.zeros_like(st_ref)

    def chunk(c, carry):
        r0 = pl.multiple_of(c * C, C)
        z = f_ref[pl.ds(r0, C), :].astype(F32)
        qp = q_ref[pl.ds(r0, C), :].astype(F32)
        v = i_ref[pl.ds(r0, C), :]
        og = g_ref[pl.ds(r0, C), :].astype(F32)
        gs = one_m * _sigmoid(z)
        lf2 = jnp.log2(lb_tiny + gs)
        kk = one_m - gs
        q = _silu(qp)
        lf_hi = lf2.astype(BF16)
        lf_lo = (lf2 - lf_hi.astype(F32)).astype(BF16)
        ex = jnp.exp2(_dot(mst_ref[...], jnp.concatenate([lf_hi, lf_lo], axis=0)))
        for hh in range(hp):
            sl = slice(hh * LANES, (hh + 1) * LANES)
            q_h, k_h, v_h = q[:, sl], kk[:, sl], v[:, sl]
            att = None
            for li in range(NL):
                e_l = ex[li * C:(li + 1) * C, sl]
                s_l = msk_ref[li] * _dot_nt((q_h * e_l).astype(BF16), (k_h * e_l).astype(BF16))
                att = s_l if att is None else att + s_l
            diag = jnp.sum(q_h * k_h, axis=-1, keepdims=True)
            o = _dot(att.astype(BF16), v_h) + diag * v_h.astype(F32)
            st = st_ref[hh]
            q_in = (q_h * ex[NL * C:(NL + 1) * C, sl]).astype(BF16)
            o = o + _dot_nt(q_in, st.astype(BF16))
            k_in = (k_h * ex[(NL + 1) * C:(NL + 2) * C, sl]).astype(BF16)
            d_row = ex[(NL + 1) * C - 1:(NL + 1) * C, sl]
            st_ref[hh] = st * d_row + _dot_tn(v_h, k_in)
            y = _rms(o, nw) * _silu(og[:, sl])
            o_ref[pl.ds(r0, C), sl] = y.astype(o_ref.dtype)
        return carry

    lax.fori_loop(0, nchunk, chunk, 0, unroll=2)


def _hgrn(a_proj, lb_logits, norm_w, layer, hp=4):
    Bsz, S, _ = a_proj.shape
    mst, msk = _hgrn_constants()
    bw = hp * A_DIM
    nhb = A_WIDTH // bw
    specs = [pl.BlockSpec((None, S, bw), lambda b, h, k=k: (b, 0, k * nhb + h)) for k in range(4)]
    return pl.pallas_call(
        functools.partial(_hgrn_kernel, layer=layer, hp=hp, nchunk=S // HGRN_CHUNK),
        grid=(Bsz, nhb),
        in_specs=specs + [
            pl.BlockSpec((DEPTH, bw), lambda b, h: (0, h)),
            pl.BlockSpec((1, A_DIM), lambda b, h: (0, 0)),
            pl.BlockSpec(mst.shape, lambda b, h: (0, 0)),
            pl.BlockSpec(msk.shape, lambda b, h: (0, 0, 0)),
        ],
        out_specs=pl.BlockSpec((None, S, bw), lambda b, h: (b, 0, h)),
        out_shape=jax.ShapeDtypeStruct((Bsz, S, A_WIDTH), BF16),
        scratch_shapes=[pltpu.VMEM((hp, A_DIM, A_DIM), F32)],
        compiler_params=_cparams("arbitrary", "arbitrary"),
        name="hgrn2",
    )(a_proj, a_proj, a_proj, a_proj, lb_logits, norm_w.reshape(1, A_DIM),
      jnp.asarray(mst, BF16), jnp.asarray(msk, F32))


def _bias_indices():
    Q = B_BLOCK
    steps = np.arange(Q)[:, None] + Q - np.arange(2 * Q)[None, :]
    valid = (steps >= 0) & (steps <= Q)
    exact = REL_BUCKETS // 2
    idx = []
    for _, dil in B_GROUPS:
        dist = np.clip(steps, 0, Q) * dil
        d = np.maximum(dist, 1).astype(np.float32)
        large = exact + (np.log(d / np.float32(exact)) / np.float32(math.log(REL_MAX_DIST / exact))
                         * np.float32(REL_BUCKETS - exact)).astype(np.int32)
        idx.append(np.where(dist < exact, dist, np.clip(large, exact, REL_BUCKETS - 1)))
    return np.stack(idx, axis=0).astype(np.int32)


def _build_bias_tables(idx_ref, rel_ref, tab_ref, pair):
    Q = B_BLOCK
    qi = lax.broadcasted_iota(jnp.int32, (Q, 2 * Q), 0)
    ki = lax.broadcasted_iota(jnp.int32, (Q, 2 * Q), 1)
    steps = qi + Q - ki
    valid = (steps >= 0) & (steps <= Q)
    valid_first = valid & (ki >= Q)
    for g in range(len(B_GROUPS)):
        idx = idx_ref[g]
        for s in range(2):
            head = g * B_SLOTS + 2 * pair + s
            bias = jnp.zeros((Q, 2 * Q), F32)
            for b in range(REL_BUCKETS):
                bias = jnp.where(idx == b, rel_ref[b, head], bias)
            bias = bias * LOG2E
            tab_ref[g, s, 0] = jnp.where(valid_first, bias, MASK_VALUE)
            tab_ref[g, s, 1] = jnp.where(valid, bias, MASK_VALUE)


ATTN_UNROLL = 4
ATTN_SUB = 4


def _attn_kernel(*refs, seq):
    ng = len(B_GROUPS)
    qkv = refs[:3 * ng]
    idx_ref, rel_ref, o_ref = refs[3 * ng:3 * ng + 3]
    tab_ref, tmp_ref, tmp2_ref, qd_ref, kd_ref, vd_ref = refs[3 * ng + 3:3 * ng + 9]
    res = refs[3 * ng + 9:]
    Q = B_BLOCK
    lane = lax.broadcasted_iota(jnp.int32, (Q, LANES), 1)
    low = lane < B_HEAD_DIM
    _build_bias_tables(idx_ref, rel_ref, tab_ref, pl.program_id(1))
    kd_ref[0:Q, :] = jnp.zeros((Q, LANES), F32)
    vd_ref[0:Q, :] = jnp.zeros((Q, LANES), F32)
    quarter = seq // ATTN_SUB

    for g, (_, dil) in enumerate(B_GROUPS):
        nblk = seq // dil // Q
        for src, dst in zip(qkv[3 * g:3 * g + 3], (qd_ref, kd_ref, vd_ref)):
            if dil == 1:
                dst[Q:Q + seq, :] = src[...].astype(F32)
                continue
            tmp_ref[...] = src[...].astype(F32)
            mid = dst if dil == ATTN_SUB else tmp2_ref
            off = Q if dil == ATTN_SUB else 0
            for r1 in range(ATTN_SUB):
                mid[off + r1 * quarter:off + (r1 + 1) * quarter, :] = tmp_ref[pl.ds(r1, quarter, stride=ATTN_SUB), :]
            if dil != ATTN_SUB:
                for r in range(ATTN_SUB * ATTN_SUB):
                    r1, r2 = divmod(r, ATTN_SUB)
                    dst[Q + r * Q:Q + (r + 1) * Q, :] = tmp2_ref[pl.ds(r1 * quarter + r2, Q, stride=ATTN_SUB), :]
        mo, lo, ao = res[3 * g:3 * g + 3]

        def units(it, carry, g=g, nblk=nblk, mo=mo, lo=lo, ao=ao):
            for uu in range(ATTN_UNROLL):
                u = it * ATTN_UNROLL + uu
                base = pl.multiple_of(u * Q, Q)
                var = jnp.where(u % nblk == 0, 0, 1)
                qb = qd_ref[pl.ds(base + Q, Q), :] * (B_HEAD_DIM ** -0.5 * LOG2E)
                kb = kd_ref[pl.ds(base, 2 * Q), :].astype(BF16)
                vb = vd_ref[pl.ds(base, 2 * Q), :].astype(BF16)
                ms, ls, pv = [], [], []
                for s in range(2):
                    qs = jnp.where(low if s == 0 else ~low, qb, 0.0).astype(BF16)
                    sc = _dot_nt(qs, kb) + tab_ref[g, s, var]
                    m = jnp.max(sc, axis=-1, keepdims=True)
                    p = jnp.exp2(sc - m)
                    ms.append(m)
                    ls.append(jnp.sum(p, axis=-1, keepdims=True))
                    pv.append(_dot(p.astype(BF16), vb))
                mo[pl.ds(base, Q), :] = jnp.where(low, ms[0], ms[1])
                lo[pl.ds(base, Q), :] = jnp.where(low, ls[0], ls[1])
                ao[pl.ds(base, Q), :] = jnp.where(low, pv[0], pv[1])
            return carry

        lax.fori_loop(0, seq // Q // ATTN_UNROLL, units, 0)

    def merge(dst, other):
        m_a, m_b = dst[0][...], other[0][...]
        mx = jnp.maximum(m_a, m_b)
        wa, wb = jnp.exp2(m_a - mx), jnp.exp2(m_b - mx)
        dst[0][...] = mx
        dst[1][...] = wa * dst[1][...] + wb * other[1][...]
        dst[2][...] = wa * dst[2][...] + wb * other[2][...]

    g1, g4, g16, spare = res[0:3], res[3:6], res[6:9], res[9:12]
    for src, dst in zip(g16, spare):
        for r in range(ATTN_SUB * ATTN_SUB):
            r1, r2 = divmod(r, ATTN_SUB)
            dst[pl.ds(r1 * quarter + r2, Q, stride=ATTN_SUB), :] = src[r * Q:(r + 1) * Q, :]
    merge(g4, spare)
    for src, dst in zip(g4, spare):
        for r1 in range(ATTN_SUB):
            dst[pl.ds(r1, quarter, stride=ATTN_SUB), :] = src[r1 * quarter:(r1 + 1) * quarter, :]
    merge(g1, spare)
    o_ref[...] = (g1[2][...] / g1[1][...]).astype(o_ref.dtype)


def _attn(bqkv, rel_bias):
    _, Bsz, S, _ = bqkv.shape
    ng = len(B_GROUPS)
    pairs = B_SLOTS // 2
    assert [d for _, d in B_GROUPS] == [1, ATTN_SUB, ATTN_SUB * ATTN_SUB]
    assert S % (ATTN_SUB * ATTN_SUB * B_BLOCK) == 0 and (S // B_BLOCK) % ATTN_UNROLL == 0
    idx = jnp.asarray(_bias_indices())
    in_specs, args = [], []
    for g in range(ng):
        for a in range(3):
            in_specs.append(pl.BlockSpec((None, None, S, LANES), lambda b, p, a=a, g=g: (a, b, 0, g * pairs + p)))
            args.append(bqkv)
    in_specs.append(pl.BlockSpec(idx.shape, lambda b, p: (0, 0, 0)))
    in_specs.append(pl.BlockSpec(memory_space=pltpu.SMEM))
    big = pltpu.VMEM((S, LANES), F32)
    pad = pltpu.VMEM((S + B_BLOCK, LANES), F32)
    return pl.pallas_call(
        functools.partial(_attn_kernel, seq=S),
        grid=(Bsz, pairs),
        in_specs=in_specs,
        out_specs=pl.BlockSpec((None, S, LANES), lambda b, p: (b, 0, p)),
        out_shape=jax.ShapeDtypeStruct((Bsz, S, B_OUT), BF16),
        scratch_shapes=[pltpu.VMEM((ng, 2, 2, B_BLOCK, 2 * B_BLOCK), F32), big, big, pad, pad, pad] + [big] * 12,
        compiler_params=_cparams("arbitrary", "arbitrary"),
        name="dilated_attn",
    )(*args, idx, rel_bias.astype(F32))


CONV_HALO = 32
CONV_ROWS = 128
CONV_LN_ROWS = 32


def _conv_kernel(a_ref, g_ref, w_ref, cb_ref, lg_ref, lbias_ref, o_ref, u_ref, y_ref, *, ts):
    @pl.when(pl.program_id(1) == 0)
    def _():
        u_ref[0:CONV_HALO, :] = jnp.zeros((CONV_HALO, C_WIDTH), F32)

    u_ref[CONV_HALO:CONV_HALO + ts, :] = a_ref[...].astype(F32) * _sigmoid(g_ref[...].astype(F32))
    off = CONV_HALO - (C_KERNEL - 1)
    nwin = CONV_ROWS + CONV_HALO

    def rows(c, carry):
        r0 = pl.multiple_of(c * CONV_ROWS, CONV_ROWS)
        for cbk in range(C_WIDTH // LANES):
            ln = slice(cbk * LANES, (cbk + 1) * LANES)
            win = u_ref[pl.ds(r0, nwin), ln]
            acc = jnp.zeros((CONV_ROWS, LANES), F32)
            for s in range(8):
                taps = [j for j in range(C_KERNEL) if (off + j) % 8 == s]
                sh = win if s == 0 else pltpu.roll(win, nwin - s, 0)
                for j in taps:
                    a0 = off + j - s
                    acc = acc + w_ref[j:j + 1, ln] * sh[a0:a0 + CONV_ROWS, :]
            y_ref[:, ln] = acc + cb_ref[:, ln]
        for k in range(CONV_ROWS // CONV_LN_ROWS):
            y = y_ref[k * CONV_LN_ROWS:(k + 1) * CONV_LN_ROWS, :]
            mu = jnp.mean(y, axis=-1, keepdims=True)
            yc = y - mu
            var = jnp.mean(yc * yc, axis=-1, keepdims=True)
            o_ref[pl.ds(r0 + k * CONV_LN_ROWS, CONV_LN_ROWS), :] = _silu(
                yc * lax.rsqrt(var + EPS) * lg_ref[...] + lbias_ref[...]).astype(o_ref.dtype)
        return carry

    lax.fori_loop(0, ts // CONV_ROWS, rows, 0)
    u_ref[0:CONV_HALO, :] = u_ref[ts:ts + CONV_HALO, :]


def _conv(ca, cg, conv_w, conv_b, ln_g, ln_b):
    (arr_a, ia), (arr_g, ig) = ca, cg
    _, Bsz, S, C = arr_a.shape
    ts = min(S, 512)
    wpad = jnp.zeros((CONV_HALO, C), F32).at[:C_KERNEL].set(conv_w)
    vec = lambda v: v.reshape(1, C)
    vspec = pl.BlockSpec((1, C), lambda b, i: (0, 0))
    return pl.pallas_call(
        functools.partial(_conv_kernel, ts=ts),
        grid=(Bsz, S // ts),
        in_specs=[pl.BlockSpec((None, None, ts, C), lambda b, i: (ia, b, i, 0)),
                  pl.BlockSpec((None, None, ts, C), lambda b, i: (ig, b, i, 0)),
                  pl.BlockSpec((CONV_HALO, C), lambda b, i: (0, 0)),
                  vspec, vspec, vspec],
        out_specs=pl.BlockSpec((None, ts, C), lambda b, i: (b, i, 0)),
        out_shape=jax.ShapeDtypeStruct((Bsz, S, C), BF16),
        scratch_shapes=[pltpu.VMEM((CONV_HALO + ts, C), F32), pltpu.VMEM((CONV_ROWS, C), F32)],
        compiler_params=_cparams("arbitrary", "arbitrary"),
        name="conv_module",
    )(arr_a, arr_g, wpad, vec(conv_b), vec(ln_g), vec(ln_b))


def _mix_out_kernel(ya_ref, yb_ref, yc_ref, g0_ref, g1_ref, g2_ref, wa_ref, wb_ref, wc_ref, wo_ref,
                    nw_ref, mod_ref, o_ref, acc_ref, *, gate_row):
    j = pl.program_id(2)
    last_j = pl.num_programs(2) - 1
    tm = o_ref.shape[0]

    def step(rb, first, last):
        for r in range(tm // rb):
            rs = slice(r * rb, (r + 1) * rb)
            m = (g0_ref[rs, :].astype(F32) * _dot(ya_ref[rs, :], wa_ref[...])
                 + g1_ref[rs, :].astype(F32) * _dot(yb_ref[rs, :], wb_ref[...])
                 + g2_ref[rs, :].astype(F32) * _dot(yc_ref[rs, :], wc_ref[...]))
            y = _dot(m.astype(BF16), wo_ref[...])
            acc = y if first else acc_ref[rs, :] + y
            if last:
                o_ref[rs, :] = mod_ref[gate_row:gate_row + 1, :] * _rms(acc, nw_ref[...])
            else:
                acc_ref[rs, :] = acc

    pl.when(j == 0)(functools.partial(step, tm, True, False))
    pl.when((j > 0) & (j < last_j))(functools.partial(step, tm, False, False))
    pl.when(j == last_j)(functools.partial(step, min(tm, EPI_ROWS), False, True))


def _mix_out(ya, yb, yc, gates, wa, wb, wc, wo, nw, mod, layer, gate_row):
    Bsz, S, _ = ya.shape
    D = wo.shape[-1]
    tm = min(S, 1024)
    tn = 512
    nj = D // tn
    tok = lambda width: pl.BlockSpec((None, tm, width), lambda b, i, j: (b, i, 0))
    gspec = lambda k: pl.BlockSpec((None, tm, tn), lambda b, i, j, k=k: (b, i, k * nj + j))
    wspec = lambda K: pl.BlockSpec((None, K, tn), lambda b, i, j: (layer, 0, j))
    return pl.pallas_call(
        functools.partial(_mix_out_kernel, gate_row=gate_row),
        grid=(Bsz, S // tm, nj),
        in_specs=[tok(ya.shape[-1]), tok(yb.shape[-1]), tok(yc.shape[-1]),
                  gspec(0), gspec(1), gspec(2),
                  wspec(wa.shape[1]), wspec(wb.shape[1]), wspec(wc.shape[1]),
                  pl.BlockSpec((None, tn, D), lambda b, i, j: (layer, j, 0)),
                  pl.BlockSpec((1, D), lambda b, i, j: (0, 0)),
                  pl.BlockSpec((None, 6, D), lambda b, i, j: (b, 0, 0))],
        out_specs=tok(D),
        out_shape=jax.ShapeDtypeStruct((Bsz, S, D), F32),
        scratch_shapes=[pltpu.VMEM((tm, D), F32)],
        compiler_params=_cparams("arbitrary", "arbitrary", "arbitrary"),
        name="mix_out",
    )(ya, yb, yc, gates, gates, gates, wa, wb, wc, wo, nw.reshape(1, D), mod)


def _mlp_kernel(x_ref, d_ref, wu_ref, wd_ref, npre_ref, npost_ref, mod_ref, o_ref, xn_ref, h_ref, acc_ref,
                *, sh_row, sc_row, gate_row):
    j = pl.program_id(2)
    last_j = pl.num_programs(2) - 1
    tm = o_ref.shape[0]

    def step(rb, first, last):
        for r in range(tm // rb):
            rs = slice(r * rb, (r + 1) * rb)
            if first:
                xn = x_ref[rs, :] + d_ref[rs, :]
                xn_ref[rs, :] = xn
                y = _rms(xn, npre_ref[...])
                h = (y * (1.0 + mod_ref[sc_row:sc_row + 1, :]) + mod_ref[sh_row:sh_row + 1, :]).astype(h_ref.dtype)
                h_ref[rs, :] = h
            else:
                h = h_ref[rs, :]
            u = jnp.maximum(_dot(h, wu_ref[...]), 0.0)
            y = _dot((u * u).astype(BF16), wd_ref[...])
            acc = y if first else acc_ref[rs, :] + y
            if last:
                o_ref[rs, :] = xn_ref[rs, :] + mod_ref[gate_row:gate_row + 1, :] * _rms(acc, npost_ref[...])
            else:
                acc_ref[rs, :] = acc

    pl.when(j == 0)(functools.partial(step, min(tm, EPI_ROWS), True, False))
    pl.when((j > 0) & (j < last_j))(functools.partial(step, tm, False, False))
    pl.when(j == last_j)(functools.partial(step, min(tm, EPI_ROWS), False, True))


def _mlp(x, delta, wu, wd, npre, npost, mod, layer):
    Bsz, S, D = x.shape
    FF = wu.shape[-1]
    tm = min(S, 512)
    tf = min(FF, 1024)
    tok = pl.BlockSpec((None, tm, D), lambda b, i, j: (b, i, 0))
    vspec = pl.BlockSpec((1, D), lambda b, i, j: (0, 0))
    return pl.pallas_call(
        functools.partial(_mlp_kernel, sh_row=3, sc_row=4, gate_row=5),
        grid=(Bsz, S // tm, FF // tf),
        in_specs=[tok, tok,
                  pl.BlockSpec((None, D, tf), lambda b, i, j: (layer, 0, j)),
                  pl.BlockSpec((None, tf, D), lambda b, i, j: (layer, j, 0)),
                  vspec, vspec,
                  pl.BlockSpec((None, 6, D), lambda b, i, j: (b, 0, 0))],
        out_specs=tok,
        out_shape=jax.ShapeDtypeStruct((Bsz, S, D), F32),
        scratch_shapes=[pltpu.VMEM((tm, D), F32), pltpu.VMEM((tm, D), BF16), pltpu.VMEM((tm, D), F32)],
        compiler_params=_cparams("arbitrary", "arbitrary", "arbitrary"),
        name="mlp",
    )(x, delta, wu, wd, npre.reshape(1, D), npost.reshape(1, D), mod)


def kernel(x, c, rel_bias, hgrn_lb_logits, w_ada, b_ada, mix_norm_pre, mix_norm_post, w_in, w_gate, b_gate,
           hgrn_norm_w, conv_w, conv_b, conv_ln_g, conv_ln_b, w_a_out, w_b_out, w_c_out, w_o, mlp_norm_pre,
           mlp_norm_post, w_up, w_down):
    Bsz, S, D = x.shape
    T = Bsz * S
    mod = _ada(c, w_ada, b_ada)
    wa, wb, wc, wo = _cast_bf16(w_a_out), _cast_bf16(w_b_out), _cast_bf16(w_c_out), _cast_bf16(w_o)
    wu, wd = _cast_bf16(w_up), _cast_bf16(w_down)
    lb_logits = hgrn_lb_logits.astype(F32)
    for l in range(DEPTH):
        h = _prenorm(x, mix_norm_pre[l], mod[l], 0, 1).reshape(T, D)
        a_proj = _proj(h, w_in, l, nb=1, bw=1024, col0=0, nsteps=4)
        bc = _proj(h, w_in, l, nb=3, bw=256, col0=4 * A_WIDTH, nsteps=5, stacked_out=True)
        gates = _proj(h, w_gate, l, nb=1, bw=1024, col0=0, nsteps=N_BRANCH * D // 1024, bias=b_gate)
        bc = bc.reshape(5, Bsz, S, C_WIDTH)
        ya = _hgrn(a_proj.reshape(Bsz, S, 4 * A_WIDTH), lb_logits, hgrn_norm_w[l], l)
        yb = _attn(bc, rel_bias)
        yc = _conv((bc, 3), (bc, 4), conv_w[l], conv_b[l], conv_ln_g[l], conv_ln_b[l])
        delta = _mix_out(ya, yb, yc, gates.reshape(Bsz, S, N_BRANCH * D), wa, wb, wc, wo,
                         mix_norm_post[l], mod[l], l, 2)
        x = _mlp(x, delta, wu, wd, mlp_norm_pre[l], mlp_norm_post[l], mod[l], l)
    return x
```

```python
import functools
import math

import numpy as np
import jax
import jax.numpy as jnp
from jax import lax
from jax.experimental import pallas as pl
from jax.experimental.pallas import tpu as pltpu

F32 = jnp.float32
BF16 = jnp.bfloat16

D_MODEL = 2048
DEPTH = 2
A_HEADS = 8
A_DIM = 128
A_WIDTH = A_HEADS * A_DIM
B_GROUPS = ((128, 1), (512, 4), (2048, 16))
B_SLOTS = 4
B_HEAD_DIM = 64
B_WIDTH = B_SLOTS * len(B_GROUPS) * B_HEAD_DIM
B_OUT = B_SLOTS * B_HEAD_DIM
B_BLOCK = 128
C_WIDTH = 768
C_KERNEL = 31
REL_BUCKETS = 32
REL_MAX_DIST = 2048
D_FF = 4 * D_MODEL
N_BRANCH = 3
IN_WIDTH = 4 * A_WIDTH + 3 * B_WIDTH + 2 * C_WIDTH
EPS = 1e-6
MASK_VALUE = -1e30
TINY = 1e-30
LOG2E = math.log2(math.e)

LANES = 128
MXU_WIDTH = 256
VMEM_LIMIT_BYTES = 60 * 1024 * 1024

PROJ_ROWS = 256
EPI_ROWS = 256
HGRN_CHUNK = 128
HGRN_LEVELS = 7


def _cparams(*sem):
    return pltpu.CompilerParams(dimension_semantics=sem, vmem_limit_bytes=VMEM_LIMIT_BYTES)


def _dot(a, b):
    return jnp.dot(a, b, preferred_element_type=F32)


def _dot_nt(a, b):
    return lax.dot_general(a, b, (((1,), (1,)), ((), ())), preferred_element_type=F32)


def _dot_tn(a, b):
    return lax.dot_general(a, b, (((0,), (0,)), ((), ())), preferred_element_type=F32)


def _sigmoid(x):
    return 1.0 / (1.0 + jnp.exp2(x * (-LOG2E)))


def _silu(x):
    return x * _sigmoid(x)


def _rms(x, w):
    return x * lax.rsqrt(jnp.mean(x * x, axis=-1, keepdims=True) + EPS) * w


def _cast_kernel(x_ref, o_ref):
    o_ref[...] = x_ref[...].astype(o_ref.dtype)


def _cast_bf16(w):
    L, R, C = w.shape
    tr = next(t for t in (1024, 512, 256, 128, 64, 32, 16) if R % t == 0 and t * C * 4 <= 4 * 1024 * 1024)
    return pl.pallas_call(
        _cast_kernel,
        grid=(L, R // tr),
        in_specs=[pl.BlockSpec((None, tr, C), lambda l, i: (l, i, 0))],
        out_specs=pl.BlockSpec((None, tr, C), lambda l, i: (l, i, 0)),
        out_shape=jax.ShapeDtypeStruct(w.shape, BF16),
        compiler_params=_cparams("arbitrary", "arbitrary"),
        name="cast_bf16",
    )(w)


def _ada_kernel(c_ref, w_ref, b_ref, o_ref):
    ca = _silu(c_ref[...]).astype(BF16)
    o_ref[...] = _dot(ca, w_ref[...].astype(BF16)) + b_ref[...]


def _ada(c, w_ada, b_ada):
    Bsz, D = c.shape
    L, _, N = w_ada.shape
    rows = 16
    cp = jnp.zeros((rows, D), F32).at[:Bsz].set(c)
    tn = 1024
    out = pl.pallas_call(
        _ada_kernel,
        grid=(L, N // tn),
        in_specs=[pl.BlockSpec((rows, D), lambda l, j: (0, 0)),
                  pl.BlockSpec((None, D, tn), lambda l, j: (l, 0, j)),
                  pl.BlockSpec((None, 1, tn), lambda l, j: (l, 0, j))],
        out_specs=pl.BlockSpec((None, rows, tn), lambda l, j: (l, 0, j)),
        out_shape=jax.ShapeDtypeStruct((L, rows, N), F32),
        compiler_params=_cparams("arbitrary", "arbitrary"),
        name="ada_mod",
    )(cp, w_ada, b_ada.reshape(L, 1, N))
    return out[:, :Bsz].reshape(L, Bsz, 6, D)


def _prenorm_kernel(x_ref, nw_ref, mod_ref, o_ref, *, sh_row, sc_row):
    y = _rms(x_ref[...], nw_ref[...])
    o_ref[...] = (y * (1.0 + mod_ref[sc_row:sc_row + 1, :]) + mod_ref[sh_row:sh_row + 1, :]).astype(o_ref.dtype)


def _prenorm(x, nw, mod, sh_row, sc_row):
    Bsz, S, D = x.shape
    ts = min(S, 512)
    return pl.pallas_call(
        functools.partial(_prenorm_kernel, sh_row=sh_row, sc_row=sc_row),
        grid=(Bsz, S // ts),
        in_specs=[pl.BlockSpec((None, ts, D), lambda b, i: (b, i, 0)),
                  pl.BlockSpec((1, D), lambda b, i: (0, 0)),
                  pl.BlockSpec((None, 6, D), lambda b, i: (b, 0, 0))],
        out_specs=pl.BlockSpec((None, ts, D), lambda b, i: (b, i, 0)),
        out_shape=jax.ShapeDtypeStruct((Bsz, S, D), BF16),
        compiler_params=_cparams("arbitrary", "arbitrary"),
        name="prenorm",
    )(x, nw.reshape(1, D), mod)


def _proj_kernel(*refs, nb, bw, gate):
    h_ref = refs[0]
    w_refs = refs[1:1 + nb]
    b_ref = refs[1 + nb] if gate else None
    o_ref = refs[-2]
    wb_ref = refs[-1]

    @pl.when(pl.program_id(1) == 0)
    def _():
        for k in range(nb):
            wb_ref[k] = w_refs[k][...].astype(BF16)

    tm = h_ref.shape[0]
    rb = min(tm, PROJ_ROWS)
    for r in range(tm // rb):
        h = h_ref[r * rb:(r + 1) * rb, :]
        for k in range(nb):
            acc = _dot(h, wb_ref[k])
            if gate:
                acc = _sigmoid(acc + b_ref[:, k * bw:(k + 1) * bw])
            o_ref[r * rb:(r + 1) * rb, k * bw:(k + 1) * bw] = acc.astype(o_ref.dtype)


def _proj(h2, w, layer, *, nb, bw, col0, nsteps, bias=None, stacked_out=False):
    T, D = h2.shape
    tm = min(T, 2048)
    tn = nb * bw
    assert col0 % bw == 0
    blk0 = col0 // bw
    in_specs = [pl.BlockSpec((tm, D), lambda j, i: (i, 0))]
    args = [h2]
    for k in range(nb):
        in_specs.append(pl.BlockSpec((None, D, bw), lambda j, i, k=k: (layer, 0, blk0 + j * nb + k)))
        args.append(w)
    if bias is not None:
        in_specs.append(pl.BlockSpec((None, 1, tn), lambda j, i: (layer, 0, j)))
        args.append(bias.reshape(bias.shape[0], 1, bias.shape[1]))
    if stacked_out:
        out_shape = jax.ShapeDtypeStruct((nsteps, T, tn), BF16)
        out_spec = pl.BlockSpec((None, tm, tn), lambda j, i: (j, i, 0))
    else:
        out_shape = jax.ShapeDtypeStruct((T, nsteps * tn), BF16)
        out_spec = pl.BlockSpec((tm, tn), lambda j, i: (i, j))
    return pl.pallas_call(
        functools.partial(_proj_kernel, nb=nb, bw=bw, gate=bias is not None),
        grid=(nsteps, T // tm),
        in_specs=in_specs,
        out_specs=out_spec,
        out_shape=out_shape,
        scratch_shapes=[pltpu.VMEM((nb, D, bw), BF16)],
        compiler_params=_cparams("arbitrary", "arbitrary"),
        name="in_proj",
    )(*args)


def _hgrn_constants():
    C = HGRN_CHUNK
    t = np.arange(C)[:, None]
    u = np.arange(C)[None, :]
    mats, masks = [], []
    for li in range(HGRN_LEVELS):
        w = 1 << li
        same = (u // (2 * w)) == (t // (2 * w))
        mid = (t // (2 * w)) * (2 * w) + w - 1
        upper = (t // w) % 2 == 1
        mats.append(np.where(upper, same & (u > mid) & (u <= t), same & (u > t) & (u <= mid)))
        masks.append(same & upper & ((u // w) % 2 == 0))
    mats.append(u <= t)
    mats.append(u > t)
    m = np.concatenate(mats, axis=0).astype(np.float32)
    return np.concatenate([m, m], axis=1), np.stack(masks, axis=0).astype(np.float32)


def _hgrn_kernel(q_ref, f_ref, i_ref, g_ref, lbl_ref, nw_ref, mst_ref, msk_ref, o_ref, st_ref, *, layer, hp, nchunk):
    C = HGRN_CHUNK
    NL = HGRN_LEVELS
    logits = lbl_ref[...]
    e = jnp.exp(logits - jnp.max(logits, axis=0, keepdims=True))
    sm = e / jnp.sum(e, axis=0, keepdims=True)
    lower = jnp.sum(sm[:layer + 1], axis=0, keepdims=True) - sm[0:1]
    lb = jnp.maximum(lower, 0.0)
    lb_tiny = lb + TINY
    one_m = 1.0 - lb
    nw = nw_ref[...]
    st_ref[...] = jnp.zeros_like(st_ref)

    def chunk(c, carry):
        r0 = pl.multiple_of(c * C, C)
        z = f_ref[pl.ds(r0, C), :].astype(F32)
        qp = q_ref[pl.ds(r0, C), :].astype(F32)
        v = i_ref[pl.ds(r0, C), :]
        og = g_ref[pl.ds(r0, C), :].astype(F32)
        gs = one_m * _sigmoid(z)
        lf2 = jnp.log2(lb_tiny + gs)
        kk = one_m - gs
        q = _silu(qp)
        lf_hi = lf2.astype(BF16)
        lf_lo = (lf2 - lf_hi.astype(F32)).astype(BF16)
        ex = jnp.exp2(_dot(mst_ref[...], jnp.concatenate([lf_hi, lf_lo], axis=0)))
        for hh in range(hp):
            sl = slice(hh * LANES, (hh + 1) * LANES)
            q_h, k_h, v_h = q[:, sl], kk[:, sl], v[:, sl]
            att = None
            for li in range(NL):
                e_l = ex[li * C:(li + 1) * C, sl]
                s_l = msk_ref[li] * _dot_nt((q_h * e_l).astype(BF16), (k_h * e_l).astype(BF16))
                att = s_l if att is None else att + s_l
            diag = jnp.sum(q_h * k_h, axis=-1, keepdims=True)
            o = _dot(att.astype(BF16), v_h) + diag * v_h.astype(F32)
            st = st_ref[hh]
            q_in = (q_h * ex[NL * C:(NL + 1) * C, sl]).astype(BF16)
            o = o + _dot_nt(q_in, st.astype(BF16))
            k_in = (k_h * ex[(NL + 1) * C:(NL + 2) * C, sl]).astype(BF16)
            d_row = ex[(NL + 1) * C - 1:(NL + 1) * C, sl]
            st_ref[hh] = st * d_row + _dot_tn(v_h, k_in)
            y = _rms(o, nw) * _silu(og[:, sl])
            o_ref[pl.ds(r0, C), sl] = y.astype(o_ref.dtype)
        return carry

    lax.fori_loop(0, nchunk, chunk, 0, unroll=2)


def _hgrn(a_proj, lb_logits, norm_w, layer, hp=4):
    Bsz, S, _ = a_proj.shape
    mst, msk = _hgrn_constants()
    bw = hp * A_DIM
    nhb = A_WIDTH // bw
    specs = [pl.BlockSpec((None, S, bw), lambda b, h, k=k: (b, 0, k * nhb + h)) for k in range(4)]
    return pl.pallas_call(
        functools.partial(_hgrn_kernel, layer=layer, hp=hp, nchunk=S // HGRN_CHUNK),
        grid=(Bsz, nhb),
        in_specs=specs + [
            pl.BlockSpec((DEPTH, bw), lambda b, h: (0, h)),
            pl.BlockSpec((1, A_DIM), lambda b, h: (0, 0)),
            pl.BlockSpec(mst.shape, lambda b, h: (0, 0)),
            pl.BlockSpec(msk.shape, lambda b, h: (0, 0, 0)),
        ],
        out_specs=pl.BlockSpec((None, S, bw), lambda b, h: (b, 0, h)),
        out_shape=jax.ShapeDtypeStruct((Bsz, S, A_WIDTH), BF16),
        scratch_shapes=[pltpu.VMEM((hp, A_DIM, A_DIM), F32)],
        compiler_params=_cparams("arbitrary", "arbitrary"),
        name="hgrn2",
    )(a_proj, a_proj, a_proj, a_proj, lb_logits, norm_w.reshape(1, A_DIM),
      jnp.asarray(mst, BF16), jnp.asarray(msk, F32))


def _bias_indices():
    Q = B_BLOCK
    steps = np.arange(Q)[:, None] + Q - np.arange(2 * Q)[None, :]
    valid = (steps >= 0) & (steps <= Q)
    exact = REL_BUCKETS // 2
    idx = []
    for _, dil in B_GROUPS:
        dist = np.clip(steps, 0, Q) * dil
        d = np.maximum(dist, 1).astype(np.float32)
        large = exact + (np.log(d / np.float32(exact)) / np.float32(math.log(REL_MAX_DIST / exact))
                         * np.float32(REL_BUCKETS - exact)).astype(np.int32)
        idx.append(np.where(dist < exact, dist, np.clip(large, exact, REL_BUCKETS - 1)))
    return np.stack(idx, axis=0).astype(np.int32)


def _bias_table_kernel(idx_ref, rel_ref, tab_ref):
    Q = B_BLOCK
    pair = pl.program_id(0)
    qi = lax.broadcasted_iota(jnp.int32, (Q, 2 * Q), 0)
    ki = lax.broadcasted_iota(jnp.int32, (Q, 2 * Q), 1)
    steps = qi + Q - ki
    valid = (steps >= 0) & (steps <= Q)
    valid_first = valid & (ki >= Q)
    for g in range(len(B_GROUPS)):
        idx = idx_ref[g]
        for s in range(2):
            head = g * B_SLOTS + 2 * pair + s
            bias = jnp.zeros((Q, 2 * Q), F32)
            for b in range(REL_BUCKETS):
                bias = jnp.where(idx == b, rel_ref[b, head], bias)
            bias = bias * LOG2E
            tab_ref[g, 0, s * Q:(s + 1) * Q, :] = jnp.where(valid_first, bias, MASK_VALUE)
            tab_ref[g, 1, s * Q:(s + 1) * Q, :] = jnp.where(valid, bias, MASK_VALUE)


def _bias_tables(rel_bias):
    ng = len(B_GROUPS)
    pairs = B_SLOTS // 2
    idx = jnp.asarray(_bias_indices())
    shape = (ng, 2, 2 * B_BLOCK, 2 * B_BLOCK)
    return pl.pallas_call(
        _bias_table_kernel,
        grid=(pairs,),
        in_specs=[pl.BlockSpec(idx.shape, lambda p: (0, 0, 0)), pl.BlockSpec(memory_space=pltpu.SMEM)],
        out_specs=pl.BlockSpec((None,) + shape, lambda p: (p, 0, 0, 0, 0)),
        out_shape=jax.ShapeDtypeStruct((pairs,) + shape, F32),
        compiler_params=_cparams("arbitrary"),
        name="bias_tables",
    )(idx, rel_bias.astype(F32))


ATTN_UNROLL = 8
ATTN_SUB = 4


def _attn_kernel(*refs, seq):
    ng = len(B_GROUPS)
    qkv = refs[:3 * ng]
    tab_ref, o_ref = refs[3 * ng:3 * ng + 2]
    tmp_ref, tmp2_ref, qd_ref, kd_ref, vd_ref = refs[3 * ng + 2:3 * ng + 7]
    res = refs[3 * ng + 7:]
    Q = B_BLOCK
    lane = lax.broadcasted_iota(jnp.int32, (Q, LANES), 1)
    low = lane < B_HEAD_DIM
    kd_ref[0:Q, :] = jnp.zeros((Q, LANES), F32)
    vd_ref[0:Q, :] = jnp.zeros((Q, LANES), F32)
    quarter = seq // ATTN_SUB

    for g, (_, dil) in enumerate(B_GROUPS):
        nblk = seq // dil // Q
        for src, dst in zip(qkv[3 * g:3 * g + 3], (qd_ref, kd_ref, vd_ref)):
            if dil == 1:
                dst[Q:Q + seq, :] = src[...].astype(F32)
                continue
            tmp_ref[...] = src[...].astype(F32)
            mid = dst if dil == ATTN_SUB else tmp2_ref
            off = Q if dil == ATTN_SUB else 0
            for r1 in range(ATTN_SUB):
                mid[off + r1 * quarter:off + (r1 + 1) * quarter, :] = tmp_ref[pl.ds(r1, quarter, stride=ATTN_SUB), :]
            if dil != ATTN_SUB:
                for r in range(ATTN_SUB * ATTN_SUB):
                    r1, r2 = divmod(r, ATTN_SUB)
                    dst[Q + r * Q:Q + (r + 1) * Q, :] = tmp2_ref[pl.ds(r1 * quarter + r2, Q, stride=ATTN_SUB), :]
        mo, lo, ao = res[3 * g:3 * g + 3]

        def units(it, carry, g=g, nblk=nblk, mo=mo, lo=lo, ao=ao):
            for uu in range(ATTN_UNROLL):
                u = it * ATTN_UNROLL + uu
                base = pl.multiple_of(u * Q, Q)
                var = jnp.where(u % nblk == 0, 0, 1)
                qb = qd_ref[pl.ds(base + Q, Q), :] * (B_HEAD_DIM ** -0.5 * LOG2E)
                kb = kd_ref[pl.ds(base, 2 * Q), :].astype(BF16)
                vb = vd_ref[pl.ds(base, 2 * Q), :].astype(BF16)
                q2 = jnp.concatenate([jnp.where(low, qb, 0.0), jnp.where(low, 0.0, qb)], axis=0).astype(BF16)
                sc = _dot_nt(q2, kb) + tab_ref[g, var]
                m = jnp.max(sc, axis=-1, keepdims=True)
                p = jnp.exp2(sc - m)
                l = jnp.sum(p, axis=-1, keepdims=True)
                pv = _dot(p.astype(BF16), vb)
                mo[pl.ds(base, Q), :] = jnp.where(low, m[:Q], m[Q:])
                lo[pl.ds(base, Q), :] = jnp.where(low, l[:Q], l[Q:])
                ao[pl.ds(base, Q), :] = jnp.where(low, pv[:Q], pv[Q:])
            return carry

        lax.fori_loop(0, seq // Q // ATTN_UNROLL, units, 0)

    def merge(dst, other):
        m_a, m_b = dst[0][...], other[0][...]
        mx = jnp.maximum(m_a, m_b)
        wa, wb = jnp.exp2(m_a - mx), jnp.exp2(m_b - mx)
        dst[0][...] = mx
        dst[1][...] = wa * dst[1][...] + wb * other[1][...]
        dst[2][...] = wa * dst[2][...] + wb * other[2][...]

    g1, g4, g16, spare = res[0:3], res[3:6], res[6:9], res[9:12]
    for src, dst in zip(g16, spare):
        for r in range(ATTN_SUB * ATTN_SUB):
            r1, r2 = divmod(r, ATTN_SUB)
            dst[pl.ds(r1 * quarter + r2, Q, stride=ATTN_SUB), :] = src[r * Q:(r + 1) * Q, :]
    merge(g4, spare)
    for src, dst in zip(g4, spare):
        for r1 in range(ATTN_SUB):
            dst[pl.ds(r1, quarter, stride=ATTN_SUB), :] = src[r1 * quarter:(r1 + 1) * quarter, :]
    merge(g1, spare)
    o_ref[...] = (g1[2][...] / g1[1][...]).astype(o_ref.dtype)


def _attn(bqkv, tabs):
    _, Bsz, S, _ = bqkv.shape
    ng = len(B_GROUPS)
    pairs = B_SLOTS // 2
    assert [d for _, d in B_GROUPS] == [1, ATTN_SUB, ATTN_SUB * ATTN_SUB]
    assert S % (ATTN_SUB * ATTN_SUB * B_BLOCK) == 0 and (S // B_BLOCK) % ATTN_UNROLL == 0
    in_specs, args = [], []
    for g in range(ng):
        for a in range(3):
            in_specs.append(pl.BlockSpec((None, None, S, LANES), lambda b, p, a=a, g=g: (a, b, 0, g * pairs + p)))
            args.append(bqkv)
    in_specs.append(pl.BlockSpec((None,) + tabs.shape[1:], lambda b, p: (p, 0, 0, 0, 0)))
    big = pltpu.VMEM((S, LANES), F32)
    pad = pltpu.VMEM((S + B_BLOCK, LANES), F32)
    return pl.pallas_call(
        functools.partial(_attn_kernel, seq=S),
        grid=(Bsz, pairs),
        in_specs=in_specs,
        out_specs=pl.BlockSpec((None, S, LANES), lambda b, p: (b, 0, p)),
        out_shape=jax.ShapeDtypeStruct((Bsz, S, B_OUT), BF16),
        scratch_shapes=[big, big, pad, pad, pad] + [big] * 12,
        compiler_params=_cparams("arbitrary", "arbitrary"),
        name="dilated_attn",
    )(*args, tabs)


CONV_HALO = 32
CONV_ROWS = 128
CONV_LN_ROWS = 32
GC_ROWS = 256
GC_BLOCKS = 3


def _conv_pieces(u_ref, y_ref, w_ref, cb_ref, lg_ref, lbias_ref, o_ref, r0):
    off = CONV_HALO - (C_KERNEL - 1)
    nwin = CONV_ROWS + CONV_HALO

    def taps_block(cbk):
        ln = slice(cbk * LANES, (cbk + 1) * LANES)
        win = u_ref[r0:r0 + nwin, ln]
        acc = jnp.zeros((CONV_ROWS, LANES), F32)
        for s in range(8):
            taps = [j for j in range(C_KERNEL) if (off + j) % 8 == s]
            sh = win if s == 0 else pltpu.roll(win, nwin - s, 0)
            for j in taps:
                a0 = off + j - s
                acc = acc + w_ref[j:j + 1, ln] * sh[a0:a0 + CONV_ROWS, :]
        y_ref[r0:r0 + CONV_ROWS, ln] = acc + cb_ref[:, ln]

    def norm_block(k):
        rows = slice(r0 + k * CONV_LN_ROWS, r0 + (k + 1) * CONV_LN_ROWS)
        y = y_ref[rows, :]
        mu = jnp.mean(y, axis=-1, keepdims=True)
        yc = y - mu
        var = jnp.mean(yc * yc, axis=-1, keepdims=True)
        o_ref[rows, :] = _silu(yc * lax.rsqrt(var + EPS) * lg_ref[...] + lbias_ref[...]).astype(o_ref.dtype)

    return ([functools.partial(taps_block, cbk) for cbk in range(C_WIDTH // LANES)]
            + [functools.partial(norm_block, k) for k in range(CONV_ROWS // CONV_LN_ROWS)])


def _gate_conv_kernel(*refs, units_per_seq):
    h_ref = refs[0]
    w_refs = refs[1:1 + GC_BLOCKS]
    b_ref, ca_ref, cg_ref, cw_ref, cb_ref, lg_ref, lbias_ref = refs[1 + GC_BLOCKS:8 + GC_BLOCKS]
    g_ref, yc_ref, wb_ref, u_ref, y_ref = refs[8 + GC_BLOCKS:]
    step = pl.program_id(0) * pl.num_programs(1) + pl.program_id(1)

    @pl.when(pl.program_id(1) == 0)
    def _():
        for k in range(GC_BLOCKS):
            wb_ref[k] = w_refs[k][...].astype(BF16)

    @pl.when(step % units_per_seq == 0)
    def _():
        u_ref[0:CONV_HALO, :] = jnp.zeros((CONV_HALO, C_WIDTH), F32)

    u_ref[CONV_HALO:CONV_HALO + GC_ROWS, :] = ca_ref[...].astype(F32) * _sigmoid(cg_ref[...].astype(F32))
    pieces = []
    for c in range(GC_ROWS // CONV_ROWS):
        pieces += _conv_pieces(u_ref, y_ref, cw_ref, cb_ref, lg_ref, lbias_ref, yc_ref, c * CONV_ROWS)
    tm = h_ref.shape[0]
    rb = min(tm, PROJ_ROWS)
    ndots = (tm // rb) * GC_BLOCKS
    done = 0
    for r in range(tm // rb):
        h = h_ref[r * rb:(r + 1) * rb, :]
        for k in range(GC_BLOCKS):
            cols = slice(k * MXU_WIDTH, (k + 1) * MXU_WIDTH)
            g_ref[r * rb:(r + 1) * rb, cols] = _sigmoid(_dot(h, wb_ref[k]) + b_ref[:, cols]).astype(g_ref.dtype)
            d = r * GC_BLOCKS + k + 1
            while done < len(pieces) * d // ndots:
                pieces[done]()
                done += 1
    u_ref[0:CONV_HALO, :] = u_ref[GC_ROWS:GC_ROWS + CONV_HALO, :]


def _gate_conv(h2, w_gate, b_gate, layer, bc, ia, ig, conv_w, conv_b, ln_g, ln_b):
    T, D = h2.shape
    _, Bsz, S, C = bc.shape
    N = w_gate.shape[-1]
    tn = GC_BLOCKS * MXU_WIDTH
    tm = min(T, 2048)
    ncol, nrow = N // tn, T // tm
    units_per_seq = S // GC_ROWS
    assert ncol * nrow == Bsz * units_per_seq and N % tn == 0 and T % tm == 0
    unit = lambda j, i: j * nrow + i
    in_specs = [pl.BlockSpec((tm, D), lambda j, i: (i, 0))]
    args = [h2]
    for k in range(GC_BLOCKS):
        in_specs.append(pl.BlockSpec((None, D, MXU_WIDTH), lambda j, i, k=k: (layer, 0, j * GC_BLOCKS + k)))
        args.append(w_gate)
    cspec = lambda idx: pl.BlockSpec((None, None, GC_ROWS, C),
                                     lambda j, i: (idx, unit(j, i) // units_per_seq, unit(j, i) % units_per_seq, 0))
    vspec = pl.BlockSpec((1, C), lambda j, i: (0, 0))
    in_specs += [pl.BlockSpec((None, 1, tn), lambda j, i: (layer, 0, j)), cspec(ia), cspec(ig),
                 pl.BlockSpec((CONV_HALO, C), lambda j, i: (0, 0)), vspec, vspec, vspec]
    wpad = jnp.zeros((CONV_HALO, C), F32).at[:C_KERNEL].set(conv_w)
    vec = lambda v: v.reshape(1, C)
    args += [b_gate.reshape(b_gate.shape[0], 1, N), bc, bc, wpad, vec(conv_b), vec(ln_g), vec(ln_b)]
    return pl.pallas_call(
        functools.partial(_gate_conv_kernel, units_per_seq=units_per_seq),
        grid=(ncol, nrow),
        in_specs=in_specs,
        out_specs=[pl.BlockSpec((tm, tn), lambda j, i: (i, j)),
                   pl.BlockSpec((None, GC_ROWS, C),
                                lambda j, i: (unit(j, i) // units_per_seq, unit(j, i) % units_per_seq, 0))],
        out_shape=[jax.ShapeDtypeStruct((T, N), BF16), jax.ShapeDtypeStruct((Bsz, S, C), BF16)],
        scratch_shapes=[pltpu.VMEM((GC_BLOCKS, D, MXU_WIDTH), BF16),
                        pltpu.VMEM((CONV_HALO + GC_ROWS, C), F32), pltpu.VMEM((GC_ROWS, C), F32)],
        compiler_params=_cparams("arbitrary", "arbitrary"),
        name="gate_conv",
    )(*args)


def _mix_out_kernel(ya_ref, yb_ref, yc_ref, g0_ref, g1_ref, g2_ref, wa_ref, wb_ref, wc_ref, wo_ref,
                    nw_ref, mod_ref, o_ref, acc_ref, *, gate_row):
    j = pl.program_id(2)
    last_j = pl.num_programs(2) - 1
    tm = o_ref.shape[0]

    def step(rb, first, last):
        for r in range(tm // rb):
            rs = slice(r * rb, (r + 1) * rb)
            m = (g0_ref[rs, :].astype(F32) * _dot(ya_ref[rs, :], wa_ref[...])
                 + g1_ref[rs, :].astype(F32) * _dot(yb_ref[rs, :], wb_ref[...])
                 + g2_ref[rs, :].astype(F32) * _dot(yc_ref[rs, :], wc_ref[...]))
            y = _dot(m.astype(BF16), wo_ref[...])
            acc = y if first else acc_ref[rs, :] + y
            if last:
                o_ref[rs, :] = (mod_ref[gate_row:gate_row + 1, :] * _rms(acc, nw_ref[...])).astype(o_ref.dtype)
            else:
                acc_ref[rs, :] = acc

    pl.when(j == 0)(functools.partial(step, tm, True, False))
    pl.when((j > 0) & (j < last_j))(functools.partial(step, tm, False, False))
    pl.when(j == last_j)(functools.partial(step, min(tm, EPI_ROWS), False, True))


def _mix_out(ya, yb, yc, gates, wa, wb, wc, wo, nw, mod, layer, gate_row):
    Bsz, S, _ = ya.shape
    D = wo.shape[-1]
    tm = min(S, 1024)
    tn = 512
    nj = D // tn
    tok = lambda width: pl.BlockSpec((None, tm, width), lambda b, i, j: (b, i, 0))
    gspec = lambda k: pl.BlockSpec((None, tm, tn), lambda b, i, j, k=k: (b, i, k * nj + j))
    wspec = lambda K: pl.BlockSpec((None, K, tn), lambda b, i, j: (layer, 0, j))
    return pl.pallas_call(
        functools.partial(_mix_out_kernel, gate_row=gate_row),
        grid=(Bsz, S // tm, nj),
        in_specs=[tok(ya.shape[-1]), tok(yb.shape[-1]), tok(yc.shape[-1]),
                  gspec(0), gspec(1), gspec(2),
                  wspec(wa.shape[1]), wspec(wb.shape[1]), wspec(wc.shape[1]),
                  pl.BlockSpec((None, tn, D), lambda b, i, j: (layer, j, 0)),
                  pl.BlockSpec((1, D), lambda b, i, j: (0, 0)),
                  pl.BlockSpec((None, 6, D), lambda b, i, j: (b, 0, 0))],
        out_specs=tok(D),
        out_shape=jax.ShapeDtypeStruct((Bsz, S, D), BF16),
        scratch_shapes=[pltpu.VMEM((tm, D), F32)],
        compiler_params=_cparams("arbitrary", "arbitrary", "arbitrary"),
        name="mix_out",
    )(ya, yb, yc, gates, gates, gates, wa, wb, wc, wo, nw.reshape(1, D), mod)


def _mlp_kernel(*refs, sh_row, sc_row, gate_row, emit_next):
    x_ref, d_ref, wu_ref, wd_ref, npre_ref, npost_ref, mod_ref = refs[:7]
    if emit_next:
        nnext_ref, mnext_ref, o_ref, hn_ref, xn_ref, h_ref, acc_ref = refs[7:]
    else:
        o_ref, xn_ref, h_ref, acc_ref = refs[7:]
    j = pl.program_id(2)
    last_j = pl.num_programs(2) - 1
    tm = o_ref.shape[0]

    def step(rb, first, last):
        for r in range(tm // rb):
            rs = slice(r * rb, (r + 1) * rb)
            if first:
                xn = x_ref[rs, :] + d_ref[rs, :].astype(F32)
                xn_ref[rs, :] = xn
                y = _rms(xn, npre_ref[...])
                h = (y * (1.0 + mod_ref[sc_row:sc_row + 1, :]) + mod_ref[sh_row:sh_row + 1, :]).astype(h_ref.dtype)
                h_ref[rs, :] = h
            else:
                h = h_ref[rs, :]
            u = jnp.maximum(_dot(h, wu_ref[...]), 0.0)
            y = _dot((u * u).astype(BF16), wd_ref[...])
            acc = y if first else acc_ref[rs, :] + y
            if last:
                xo = xn_ref[rs, :] + mod_ref[gate_row:gate_row + 1, :] * _rms(acc, npost_ref[...])
                o_ref[rs, :] = xo
                if emit_next:
                    hn_ref[rs, :] = (_rms(xo, nnext_ref[...]) * (1.0 + mnext_ref[1:2, :])
                                     + mnext_ref[0:1, :]).astype(hn_ref.dtype)
            else:
                acc_ref[rs, :] = acc

    pl.when(j == 0)(functools.partial(step, min(tm, EPI_ROWS), True, False))
    pl.when((j > 0) & (j < last_j))(functools.partial(step, tm, False, False))
    pl.when(j == last_j)(functools.partial(step, min(tm, EPI_ROWS), False, True))


def _mlp(x, delta, wu, wd, npre, npost, mod, layer, next_pre=None):
    Bsz, S, D = x.shape
    FF = wu.shape[-1]
    tm = min(S, 512)
    tf = min(FF, 1024)
    tok = pl.BlockSpec((None, tm, D), lambda b, i, j: (b, i, 0))
    vspec = pl.BlockSpec((1, D), lambda b, i, j: (0, 0))
    mspec = pl.BlockSpec((None, 6, D), lambda b, i, j: (b, 0, 0))
    in_specs = [tok, tok,
                pl.BlockSpec((None, D, tf), lambda b, i, j: (layer, 0, j)),
                pl.BlockSpec((None, tf, D), lambda b, i, j: (layer, j, 0)),
                vspec, vspec, mspec]
    args = [x, delta, wu, wd, npre.reshape(1, D), npost.reshape(1, D), mod]
    out_specs, out_shape = tok, jax.ShapeDtypeStruct((Bsz, S, D), F32)
    if next_pre is not None:
        in_specs += [vspec, mspec]
        args += [next_pre[0].reshape(1, D), next_pre[1]]
        out_specs, out_shape = [tok, tok], [out_shape, jax.ShapeDtypeStruct((Bsz, S, D), BF16)]
    return pl.pallas_call(
        functools.partial(_mlp_kernel, sh_row=3, sc_row=4, gate_row=5, emit_next=next_pre is not None),
        grid=(Bsz, S // tm, FF // tf),
        in_specs=in_specs,
        out_specs=out_specs,
        out_shape=out_shape,
        scratch_shapes=[pltpu.VMEM((tm, D), F32), pltpu.VMEM((tm, D), BF16), pltpu.VMEM((tm, D), F32)],
        compiler_params=_cparams("arbitrary", "arbitrary", "arbitrary"),
        name="mlp",
    )(*args)


def kernel(x, c, rel_bias, hgrn_lb_logits, w_ada, b_ada, mix_norm_pre, mix_norm_post, w_in, w_gate, b_gate,
           hgrn_norm_w, conv_w, conv_b, conv_ln_g, conv_ln_b, w_a_out, w_b_out, w_c_out, w_o, mlp_norm_pre,
           mlp_norm_post, w_up, w_down):
    Bsz, S, D = x.shape
    T = Bsz * S
    mod = _ada(c, w_ada, b_ada)
    wa, wb, wc, wo = _cast_bf16(w_a_out), _cast_bf16(w_b_out), _cast_bf16(w_c_out), _cast_bf16(w_o)
    wu, wd = _cast_bf16(w_up), _cast_bf16(w_down)
    lb_logits = hgrn_lb_logits.astype(F32)
    tabs = _bias_tables(rel_bias)
    h = _prenorm(x, mix_norm_pre[0], mod[0], 0, 1)
    for l in range(DEPTH):
        h = h.reshape(T, D)
        a_proj = _proj(h, w_in, l, nb=1, bw=1024, col0=0, nsteps=4)
        bc = _proj(h, w_in, l, nb=3, bw=256, col0=4 * A_WIDTH, nsteps=5, stacked_out=True)
        bc = bc.reshape(5, Bsz, S, C_WIDTH)
        gates, yc = _gate_conv(h, w_gate, b_gate, l, bc, 3, 4, conv_w[l], conv_b[l], conv_ln_g[l], conv_ln_b[l])
        ya = _hgrn(a_proj.reshape(Bsz, S, 4 * A_WIDTH), lb_logits, hgrn_norm_w[l], l)
        yb = _attn(bc, tabs)
        delta = _mix_out(ya, yb, yc, gates.reshape(Bsz, S, N_BRANCH * D), wa, wb, wc, wo,
                         mix_norm_post[l], mod[l], l, 2)
        if l + 1 < DEPTH:
            x, h = _mlp(x, delta, wu, wd, mlp_norm_pre[l], mlp_norm_post[l], mod[l], l,
                        next_pre=(mix_norm_pre[l + 1], mod[l + 1]))
        else:
            x = _mlp(x, delta, wu, wd, mlp_norm_pre[l], mlp_norm_post[l], mod[l], l)
    return x
```

```python
import functools
import math

import numpy as np
import jax
import jax.numpy as jnp
from jax import lax
from jax.experimental import pallas as pl
from jax.experimental.pallas import tpu as pltpu

F32 = jnp.float32
BF16 = jnp.bfloat16

D_MODEL = 2048
DEPTH = 2
A_HEADS = 8
A_DIM = 128
A_WIDTH = A_HEADS * A_DIM
B_GROUPS = ((128, 1), (512, 4), (2048, 16))
B_SLOTS = 4
B_HEAD_DIM = 64
B_WIDTH = B_SLOTS * len(B_GROUPS) * B_HEAD_DIM
B_OUT = B_SLOTS * B_HEAD_DIM
B_BLOCK = 128
C_WIDTH = 768
C_KERNEL = 31
REL_BUCKETS = 32
REL_MAX_DIST = 2048
D_FF = 4 * D_MODEL
N_BRANCH = 3
IN_WIDTH = 4 * A_WIDTH + 3 * B_WIDTH + 2 * C_WIDTH
EPS = 1e-6
MASK_VALUE = -1e30
TINY = 1e-30
LOG2E = math.log2(math.e)

LANES = 128
MXU_WIDTH = 256
VMEM_LIMIT_BYTES = 60 * 1024 * 1024

PROJ_ROWS = 256
EPI_ROWS = 256
HGRN_CHUNK = 128
HGRN_LEVELS = 7


def _cparams(*sem):
    return pltpu.CompilerParams(dimension_semantics=sem, vmem_limit_bytes=VMEM_LIMIT_BYTES)


def _dot(a, b):
    return jnp.dot(a, b, preferred_element_type=F32)


def _dot_nt(a, b):
    return lax.dot_general(a, b, (((1,), (1,)), ((), ())), preferred_element_type=F32)


def _dot_tn(a, b):
    return lax.dot_general(a, b, (((0,), (0,)), ((), ())), preferred_element_type=F32)


def _sigmoid(x):
    return 1.0 / (1.0 + jnp.exp2(x * (-LOG2E)))


def _silu(x):
    return x * _sigmoid(x)


def _rms(x, w):
    return x * lax.rsqrt(jnp.mean(x * x, axis=-1, keepdims=True) + EPS) * w


def _ada_kernel(c_ref, w_ref, b_ref, o_ref):
    ca = _silu(c_ref[...]).astype(BF16)
    o_ref[...] = _dot(ca, w_ref[...].astype(BF16)) + b_ref[...]


def _ada(c, w_ada, b_ada):
    Bsz, D = c.shape
    L, _, N = w_ada.shape
    rows = 16
    cp = jnp.zeros((rows, D), F32).at[:Bsz].set(c)
    tn = 1024
    out = pl.pallas_call(
        _ada_kernel,
        grid=(L, N // tn),
        in_specs=[pl.BlockSpec((rows, D), lambda l, j: (0, 0)),
                  pl.BlockSpec((None, D, tn), lambda l, j: (l, 0, j)),
                  pl.BlockSpec((None, 1, tn), lambda l, j: (l, 0, j))],
        out_specs=pl.BlockSpec((None, rows, tn), lambda l, j: (l, 0, j)),
        out_shape=jax.ShapeDtypeStruct((L, rows, N), F32),
        compiler_params=_cparams("arbitrary", "arbitrary"),
        name="ada_mod",
    )(cp, w_ada, b_ada.reshape(L, 1, N))
    return out[:, :Bsz].reshape(L, Bsz, 6, D)


def _prenorm_kernel(x_ref, nw_ref, mod_ref, o_ref, *, sh_row, sc_row):
    y = _rms(x_ref[...], nw_ref[...])
    o_ref[...] = (y * (1.0 + mod_ref[sc_row:sc_row + 1, :]) + mod_ref[sh_row:sh_row + 1, :]).astype(o_ref.dtype)


def _prenorm(x, nw, mod, sh_row, sc_row):
    Bsz, S, D = x.shape
    ts = min(S, 512)
    return pl.pallas_call(
        functools.partial(_prenorm_kernel, sh_row=sh_row, sc_row=sc_row),
        grid=(Bsz, S // ts),
        in_specs=[pl.BlockSpec((None, ts, D), lambda b, i: (b, i, 0)),
                  pl.BlockSpec((1, D), lambda b, i: (0, 0)),
                  pl.BlockSpec((None, 6, D), lambda b, i: (b, 0, 0))],
        out_specs=pl.BlockSpec((None, ts, D), lambda b, i: (b, i, 0)),
        out_shape=jax.ShapeDtypeStruct((Bsz, S, D), BF16),
        compiler_params=_cparams("arbitrary", "arbitrary"),
        name="prenorm",
    )(x, nw.reshape(1, D), mod)


def _side_cast_specs(weights, layer, nsteps_total, step_of):
    in_specs, out_specs, out_shapes = [], [], []
    for w in weights:
        _, R, C = w.shape
        rows = R // nsteps_total
        assert rows * nsteps_total == R and rows % 16 == 0
        in_specs.append(pl.BlockSpec((None, rows, C), lambda j, i: (layer, step_of(j, i), 0)))
        out_specs.append(pl.BlockSpec((rows, C), lambda j, i: (step_of(j, i), 0)))
        out_shapes.append(jax.ShapeDtypeStruct((R, C), BF16))
    return in_specs, out_specs, out_shapes


def _proj_kernel(*refs, nb, bw, ncast):
    h_ref = refs[0]
    w_refs = refs[1:1 + nb]
    src_refs = refs[1 + nb:1 + nb + ncast]
    o_ref = refs[1 + nb + ncast]
    dst_refs = refs[2 + nb + ncast:2 + nb + 2 * ncast]
    wb_ref = refs[-1]

    @pl.when(pl.program_id(1) == 0)
    def _():
        for k in range(nb):
            wb_ref[k] = w_refs[k][...].astype(BF16)

    tm = h_ref.shape[0]
    rb = min(tm, PROJ_ROWS)
    for r in range(tm // rb):
        h = h_ref[r * rb:(r + 1) * rb, :]
        for k in range(nb):
            o_ref[r * rb:(r + 1) * rb, k * bw:(k + 1) * bw] = _dot(h, wb_ref[k]).astype(o_ref.dtype)
    for src, dst in zip(src_refs, dst_refs):
        dst[...] = src[...].astype(dst.dtype)


def _proj(h2, w, layer, *, nb, bw, col0, nsteps, stacked_out=False, cast=()):
    T, D = h2.shape
    tm = min(T, 2048)
    tn = nb * bw
    assert col0 % bw == 0
    blk0 = col0 // bw
    nrow = T // tm
    in_specs = [pl.BlockSpec((tm, D), lambda j, i: (i, 0))]
    args = [h2]
    for k in range(nb):
        in_specs.append(pl.BlockSpec((None, D, bw), lambda j, i, k=k: (layer, 0, blk0 + j * nb + k)))
        args.append(w)
    c_in, c_out, c_shapes = _side_cast_specs(cast, layer, nsteps * nrow, lambda j, i: j * nrow + i)
    if stacked_out:
        out_shape = jax.ShapeDtypeStruct((nsteps, T, tn), BF16)
        out_spec = pl.BlockSpec((None, tm, tn), lambda j, i: (j, i, 0))
    else:
        out_shape = jax.ShapeDtypeStruct((T, nsteps * tn), BF16)
        out_spec = pl.BlockSpec((tm, tn), lambda j, i: (i, j))
    outs = pl.pallas_call(
        functools.partial(_proj_kernel, nb=nb, bw=bw, ncast=len(cast)),
        grid=(nsteps, nrow),
        in_specs=in_specs + c_in,
        out_specs=[out_spec] + c_out,
        out_shape=[out_shape] + c_shapes,
        scratch_shapes=[pltpu.VMEM((nb, D, bw), BF16)],
        compiler_params=_cparams("arbitrary", "arbitrary"),
        name="in_proj",
    )(*args, *cast)
    return outs[0] if not cast else outs


def _hgrn_constants():
    C = HGRN_CHUNK
    t = np.arange(C)[:, None]
    u = np.arange(C)[None, :]
    mats, masks = [], []
    for li in range(HGRN_LEVELS):
        w = 1 << li
        same = (u // (2 * w)) == (t // (2 * w))
        mid = (t // (2 * w)) * (2 * w) + w - 1
        upper = (t // w) % 2 == 1
        mats.append(np.where(upper, same & (u > mid) & (u <= t), same & (u > t) & (u <= mid)))
        masks.append(same & upper & ((u // w) % 2 == 0))
    mats.append(u <= t)
    mats.append(u > t)
    m = np.concatenate(mats, axis=0).astype(np.float32)
    return np.concatenate([m, m], axis=1), np.stack(masks, axis=0).astype(np.float32)


def _hgrn_kernel(q_ref, f_ref, i_ref, g_ref, lbl_ref, nw_ref, mst_ref, msk_ref, o_ref, st_ref, *, layer, hp, nchunk):
    C = HGRN_CHUNK
    NL = HGRN_LEVELS
    logits = lbl_ref[...]
    e = jnp.exp(logits - jnp.max(logits, axis=0, keepdims=True))
    sm = e / jnp.sum(e, axis=0, keepdims=True)
    lower = jnp.sum(sm[:layer + 1], axis=0, keepdims=True) - sm[0:1]
    lb = jnp.maximum(lower, 0.0)
    lb_tiny = lb + TINY
    one_m = 1.0 - lb
    nw = nw_ref[...]
    st_ref[...] = jnp.zeros_like(st_ref)

    def chunk(c, carry):
        r0 = pl.multiple_of(c * C, C)
        z = f_ref[pl.ds(r0, C), :].astype(F32)
        qp = q_ref[pl.ds(r0, C), :].astype(F32)
        v = i_ref[pl.ds(r0, C), :]
        og = g_ref[pl.ds(r0, C), :].astype(F32)
        gs = one_m * _sigmoid(z)
        lf2 = jnp.log2(lb_tiny + gs)
        kk = one_m - gs
        q = _silu(qp)
        lf_hi = lf2.astype(BF16)
        lf_lo = (lf2 - lf_hi.astype(F32)).astype(BF16)
        ex = jnp.exp2(_dot(mst_ref[...], jnp.concatenate([lf_hi, lf_lo], axis=0)))
        for hh in range(hp):
            sl = slice(hh * LANES, (hh + 1) * LANES)
            q_h, k_h, v_h = q[:, sl], kk[:, sl], v[:, sl]
            att = None
            for li in range(NL):
                e_l = ex[li * C:(li + 1) * C, sl]
                s_l = msk_ref[li] * _dot_nt((q_h * e_l).astype(BF16), (k_h * e_l).astype(BF16))
                att = s_l if att is None else att + s_l
            diag = jnp.sum(q_h * k_h, axis=-1, keepdims=True)
            o = _dot(att.astype(BF16), v_h) + diag * v_h.astype(F32)
            st = st_ref[hh]
            q_in = (q_h * ex[NL * C:(NL + 1) * C, sl]).astype(BF16)
            o = o + _dot_nt(q_in, st.astype(BF16))
            k_in = (k_h * ex[(NL + 1) * C:(NL + 2) * C, sl]).astype(BF16)
            d_row = ex[(NL + 1) * C - 1:(NL + 1) * C, sl]
            st_ref[hh] = st * d_row + _dot_tn(v_h, k_in)
            y = _rms(o, nw) * _silu(og[:, sl])
            o_ref[pl.ds(r0, C), sl] = y.astype(o_ref.dtype)
        return carry

    lax.fori_loop(0, nchunk, chunk, 0, unroll=2)


def _hgrn(a_proj, lb_logits, norm_w, layer, hp=4):
    Bsz, S, _ = a_proj.shape
    mst, msk = _hgrn_constants()
    bw = hp * A_DIM
    nhb = A_WIDTH // bw
    specs = [pl.BlockSpec((None, S, bw), lambda b, h, k=k: (b, 0, k * nhb + h)) for k in range(4)]
    return pl.pallas_call(
        functools.partial(_hgrn_kernel, layer=layer, hp=hp, nchunk=S // HGRN_CHUNK),
        grid=(Bsz, nhb),
        in_specs=specs + [
            pl.BlockSpec((DEPTH, bw), lambda b, h: (0, h)),
            pl.BlockSpec((1, A_DIM), lambda b, h: (0, 0)),
            pl.BlockSpec(mst.shape, lambda b, h: (0, 0)),
            pl.BlockSpec(msk.shape, lambda b, h: (0, 0, 0)),
        ],
        out_specs=pl.BlockSpec((None, S, bw), lambda b, h: (b, 0, h)),
        out_shape=jax.ShapeDtypeStruct((Bsz, S, A_WIDTH), BF16),
        scratch_shapes=[pltpu.VMEM((hp, A_DIM, A_DIM), F32)],
        compiler_params=_cparams("arbitrary", "arbitrary"),
        name="hgrn2",
    )(a_proj, a_proj, a_proj, a_proj, lb_logits, norm_w.reshape(1, A_DIM),
      jnp.asarray(mst, BF16), jnp.asarray(msk, F32))


def _bias_indices():
    Q = B_BLOCK
    steps = np.arange(Q)[:, None] + Q - np.arange(2 * Q)[None, :]
    valid = (steps >= 0) & (steps <= Q)
    exact = REL_BUCKETS // 2
    idx = []
    for _, dil in B_GROUPS:
        dist = np.clip(steps, 0, Q) * dil
        d = np.maximum(dist, 1).astype(np.float32)
        large = exact + (np.log(d / np.float32(exact)) / np.float32(math.log(REL_MAX_DIST / exact))
                         * np.float32(REL_BUCKETS - exact)).astype(np.int32)
        idx.append(np.where(dist < exact, dist, np.clip(large, exact, REL_BUCKETS - 1)))
    return np.stack(idx, axis=0).astype(np.int32)


def _bias_table_kernel(idx_ref, rel_ref, tab_ref):
    Q = B_BLOCK
    pair = pl.program_id(0)
    qi = lax.broadcasted_iota(jnp.int32, (Q, 2 * Q), 0)
    ki = lax.broadcasted_iota(jnp.int32, (Q, 2 * Q), 1)
    steps = qi + Q - ki
    valid = (steps >= 0) & (steps <= Q)
    valid_first = valid & (ki >= Q)
    for g in range(len(B_GROUPS)):
        idx = idx_ref[g]
        for s in range(2):
            head = g * B_SLOTS + 2 * pair + s
            bias = jnp.zeros((Q, 2 * Q), F32)
            for b in range(REL_BUCKETS):
                bias = jnp.where(idx == b, rel_ref[b, head], bias)
            bias = bias * LOG2E
            tab_ref[g, 0, s * Q:(s + 1) * Q, :] = jnp.where(valid_first, bias, MASK_VALUE)
            tab_ref[g, 1, s * Q:(s + 1) * Q, :] = jnp.where(valid, bias, MASK_VALUE)


def _bias_tables(rel_bias):
    ng = len(B_GROUPS)
    pairs = B_SLOTS // 2
    idx = jnp.asarray(_bias_indices())
    shape = (ng, 2, 2 * B_BLOCK, 2 * B_BLOCK)
    return pl.pallas_call(
        _bias_table_kernel,
        grid=(pairs,),
        in_specs=[pl.BlockSpec(idx.shape, lambda p: (0, 0, 0)), pl.BlockSpec(memory_space=pltpu.SMEM)],
        out_specs=pl.BlockSpec((None,) + shape, lambda p: (p, 0, 0, 0, 0)),
        out_shape=jax.ShapeDtypeStruct((pairs,) + shape, F32),
        compiler_params=_cparams("arbitrary"),
        name="bias_tables",
    )(idx, rel_bias.astype(F32))


ATTN_UNROLL = 8
ATTN_SUB = 4


def _attn_kernel(*refs, seq):
    ng = len(B_GROUPS)
    qkv = refs[:3 * ng]
    tab_ref, o_ref = refs[3 * ng:3 * ng + 2]
    tmp_ref, tmp2_ref, qd_ref, kd_ref, vd_ref = refs[3 * ng + 2:3 * ng + 7]
    res = refs[3 * ng + 7:]
    Q = B_BLOCK
    lane = lax.broadcasted_iota(jnp.int32, (Q, LANES), 1)
    low = lane < B_HEAD_DIM
    kd_ref[0:Q, :] = jnp.zeros((Q, LANES), F32)
    vd_ref[0:Q, :] = jnp.zeros((Q, LANES), F32)
    quarter = seq // ATTN_SUB

    for g, (_, dil) in enumerate(B_GROUPS):
        nblk = seq // dil // Q
        for src, dst in zip(qkv[3 * g:3 * g + 3], (qd_ref, kd_ref, vd_ref)):
            if dil == 1:
                dst[Q:Q + seq, :] = src[...].astype(F32)
                continue
            tmp_ref[...] = src[...].astype(F32)
            mid = dst if dil == ATTN_SUB else tmp2_ref
            off = Q if dil == ATTN_SUB else 0
            for r1 in range(ATTN_SUB):
                mid[off + r1 * quarter:off + (r1 + 1) * quarter, :] = tmp_ref[pl.ds(r1, quarter, stride=ATTN_SUB), :]
            if dil != ATTN_SUB:
                for r in range(ATTN_SUB * ATTN_SUB):
                    r1, r2 = divmod(r, ATTN_SUB)
                    dst[Q + r * Q:Q + (r + 1) * Q, :] = tmp2_ref[pl.ds(r1 * quarter + r2, Q, stride=ATTN_SUB), :]
        mo, lo, ao = res[3 * g:3 * g + 3]

        def units(it, carry, g=g, nblk=nblk, mo=mo, lo=lo, ao=ao):
            for uu in range(ATTN_UNROLL):
                u = it * ATTN_UNROLL + uu
                base = pl.multiple_of(u * Q, Q)
                var = jnp.where(u % nblk == 0, 0, 1)
                qb = qd_ref[pl.ds(base + Q, Q), :] * (B_HEAD_DIM ** -0.5 * LOG2E)
                kb = kd_ref[pl.ds(base, 2 * Q), :].astype(BF16)
                vb = vd_ref[pl.ds(base, 2 * Q), :].astype(BF16)
                q2 = jnp.concatenate([jnp.where(low, qb, 0.0), jnp.where(low, 0.0, qb)], axis=0).astype(BF16)
                sc = _dot_nt(q2, kb) + tab_ref[g, var]
                m = jnp.max(sc, axis=-1, keepdims=True)
                p = jnp.exp2(sc - m)
                l = jnp.sum(p, axis=-1, keepdims=True)
                pv = _dot(p.astype(BF16), vb)
                mo[pl.ds(base, Q), :] = jnp.where(low, m[:Q], m[Q:])
                lo[pl.ds(base, Q), :] = jnp.where(low, l[:Q], l[Q:])
                ao[pl.ds(base, Q), :] = jnp.where(low, pv[:Q], pv[Q:])
            return carry

        lax.fori_loop(0, seq // Q // ATTN_UNROLL, units, 0)

    def merge(dst, other):
        m_a, m_b = dst[0][...], other[0][...]
        mx = jnp.maximum(m_a, m_b)
        wa, wb = jnp.exp2(m_a - mx), jnp.exp2(m_b - mx)
        dst[0][...] = mx
        dst[1][...] = wa * dst[1][...] + wb * other[1][...]
        dst[2][...] = wa * dst[2][...] + wb * other[2][...]

    g1, g4, g16, spare = res[0:3], res[3:6], res[6:9], res[9:12]
    for src, dst in zip(g16, spare):
        for r in range(ATTN_SUB * ATTN_SUB):
            r1, r2 = divmod(r, ATTN_SUB)
            dst[pl.ds(r1 * quarter + r2, Q, stride=ATTN_SUB), :] = src[r * Q:(r + 1) * Q, :]
    merge(g4, spare)
    for src, dst in zip(g4, spare):
        for r1 in range(ATTN_SUB):
            dst[pl.ds(r1, quarter, stride=ATTN_SUB), :] = src[r1 * quarter:(r1 + 1) * quarter, :]
    merge(g1, spare)
    o_ref[...] = (g1[2][...] / g1[1][...]).astype(o_ref.dtype)


def _attn(bqkv, tabs):
    _, Bsz, S, _ = bqkv.shape
    ng = len(B_GROUPS)
    pairs = B_SLOTS // 2
    assert [d for _, d in B_GROUPS] == [1, ATTN_SUB, ATTN_SUB * ATTN_SUB]
    assert S % (ATTN_SUB * ATTN_SUB * B_BLOCK) == 0 and (S // B_BLOCK) % ATTN_UNROLL == 0
    in_specs, args = [], []
    for g in range(ng):
        for a in range(3):
            in_specs.append(pl.BlockSpec((None, None, S, LANES), lambda b, p, a=a, g=g: (a, b, 0, g * pairs + p)))
            args.append(bqkv)
    in_specs.append(pl.BlockSpec((None,) + tabs.shape[1:], lambda b, p: (p, 0, 0, 0, 0)))
    big = pltpu.VMEM((S, LANES), F32)
    pad = pltpu.VMEM((S + B_BLOCK, LANES), F32)
    return pl.pallas_call(
        functools.partial(_attn_kernel, seq=S),
        grid=(Bsz, pairs),
        in_specs=in_specs,
        out_specs=pl.BlockSpec((None, S, LANES), lambda b, p: (b, 0, p)),
        out_shape=jax.ShapeDtypeStruct((Bsz, S, B_OUT), BF16),
        scratch_shapes=[big, big, pad, pad, pad] + [big] * 12,
        compiler_params=_cparams("arbitrary", "arbitrary"),
        name="dilated_attn",
    )(*args, tabs)


CONV_HALO = 32
CONV_ROWS = 128
CONV_LN_ROWS = 32
GC_ROWS = 256
GC_BLOCKS = 3


def _conv_pieces(u_ref, y_ref, w_ref, cb_ref, lg_ref, lbias_ref, o_ref, r0):
    off = CONV_HALO - (C_KERNEL - 1)
    nwin = CONV_ROWS + CONV_HALO

    def taps_block(cbk):
        ln = slice(cbk * LANES, (cbk + 1) * LANES)
        win = u_ref[r0:r0 + nwin, ln]
        acc = jnp.zeros((CONV_ROWS, LANES), F32)
        for s in range(8):
            taps = [j for j in range(C_KERNEL) if (off + j) % 8 == s]
            sh = win if s == 0 else pltpu.roll(win, nwin - s, 0)
            for j in taps:
                a0 = off + j - s
                acc = acc + w_ref[j:j + 1, ln] * sh[a0:a0 + CONV_ROWS, :]
        y_ref[r0:r0 + CONV_ROWS, ln] = acc + cb_ref[:, ln]

    def norm_block(k):
        rows = slice(r0 + k * CONV_LN_ROWS, r0 + (k + 1) * CONV_LN_ROWS)
        y = y_ref[rows, :]
        mu = jnp.mean(y, axis=-1, keepdims=True)
        yc = y - mu
        var = jnp.mean(yc * yc, axis=-1, keepdims=True)
        o_ref[rows, :] = _silu(yc * lax.rsqrt(var + EPS) * lg_ref[...] + lbias_ref[...]).astype(o_ref.dtype)

    return ([functools.partial(taps_block, cbk) for cbk in range(C_WIDTH // LANES)]
            + [functools.partial(norm_block, k) for k in range(CONV_ROWS // CONV_LN_ROWS)])


def _gate_conv_kernel(*refs, units_per_seq, ncast):
    h_ref = refs[0]
    w_refs = refs[1:1 + GC_BLOCKS]
    b_ref, ca_ref, cg_ref, cw_ref, cb_ref, lg_ref, lbias_ref = refs[1 + GC_BLOCKS:8 + GC_BLOCKS]
    src_refs = refs[8 + GC_BLOCKS:8 + GC_BLOCKS + ncast]
    g_ref, yc_ref = refs[8 + GC_BLOCKS + ncast:10 + GC_BLOCKS + ncast]
    dst_refs = refs[10 + GC_BLOCKS + ncast:10 + GC_BLOCKS + 2 * ncast]
    wb_ref, u_ref, y_ref = refs[10 + GC_BLOCKS + 2 * ncast:]
    step = pl.program_id(0) * pl.num_programs(1) + pl.program_id(1)
    for src, dst in zip(src_refs, dst_refs):
        dst[...] = src[...].astype(dst.dtype)

    @pl.when(pl.program_id(1) == 0)
    def _():
        for k in range(GC_BLOCKS):
            wb_ref[k] = w_refs[k][...].astype(BF16)

    @pl.when(step % units_per_seq == 0)
    def _():
        u_ref[0:CONV_HALO, :] = jnp.zeros((CONV_HALO, C_WIDTH), F32)

    u_ref[CONV_HALO:CONV_HALO + GC_ROWS, :] = ca_ref[...].astype(F32) * _sigmoid(cg_ref[...].astype(F32))
    pieces = []
    for c in range(GC_ROWS // CONV_ROWS):
        pieces += _conv_pieces(u_ref, y_ref, cw_ref, cb_ref, lg_ref, lbias_ref, yc_ref, c * CONV_ROWS)
    tm = h_ref.shape[0]
    rb = min(tm, PROJ_ROWS)
    ndots = (tm // rb) * GC_BLOCKS
    done = 0
    for r in range(tm // rb):
        h = h_ref[r * rb:(r + 1) * rb, :]
        for k in range(GC_BLOCKS):
            cols = slice(k * MXU_WIDTH, (k + 1) * MXU_WIDTH)
            g_ref[r * rb:(r + 1) * rb, cols] = _sigmoid(_dot(h, wb_ref[k]) + b_ref[:, cols]).astype(g_ref.dtype)
            d = r * GC_BLOCKS + k + 1
            while done < len(pieces) * d // ndots:
                pieces[done]()
                done += 1
    u_ref[0:CONV_HALO, :] = u_ref[GC_ROWS:GC_ROWS + CONV_HALO, :]


def _gate_conv(h2, w_gate, b_gate, layer, bc, ia, ig, conv_w, conv_b, ln_g, ln_b, cast=()):
    T, D = h2.shape
    _, Bsz, S, C = bc.shape
    N = w_gate.shape[-1]
    tn = GC_BLOCKS * MXU_WIDTH
    tm = min(T, 2048)
    ncol, nrow = N // tn, T // tm
    units_per_seq = S // GC_ROWS
    assert ncol * nrow == Bsz * units_per_seq and N % tn == 0 and T % tm == 0
    unit = lambda j, i: j * nrow + i
    in_specs = [pl.BlockSpec((tm, D), lambda j, i: (i, 0))]
    args = [h2]
    for k in range(GC_BLOCKS):
        in_specs.append(pl.BlockSpec((None, D, MXU_WIDTH), lambda j, i, k=k: (layer, 0, j * GC_BLOCKS + k)))
        args.append(w_gate)
    cspec = lambda idx: pl.BlockSpec((None, None, GC_ROWS, C),
                                     lambda j, i: (idx, unit(j, i) // units_per_seq, unit(j, i) % units_per_seq, 0))
    vspec = pl.BlockSpec((1, C), lambda j, i: (0, 0))
    in_specs += [pl.BlockSpec((None, 1, tn), lambda j, i: (layer, 0, j)), cspec(ia), cspec(ig),
                 pl.BlockSpec((CONV_HALO, C), lambda j, i: (0, 0)), vspec, vspec, vspec]
    wpad = jnp.zeros((CONV_HALO, C), F32).at[:C_KERNEL].set(conv_w)
    vec = lambda v: v.reshape(1, C)
    args += [b_gate.reshape(b_gate.shape[0], 1, N), bc, bc, wpad, vec(conv_b), vec(ln_g), vec(ln_b)]
    c_in, c_out, c_shapes = _side_cast_specs(cast, layer, ncol * nrow, unit)
    return pl.pallas_call(
        functools.partial(_gate_conv_kernel, units_per_seq=units_per_seq, ncast=len(cast)),
        grid=(ncol, nrow),
        in_specs=in_specs + c_in,
        out_specs=[pl.BlockSpec((tm, tn), lambda j, i: (i, j)),
                   pl.BlockSpec((None, GC_ROWS, C),
                                lambda j, i: (unit(j, i) // units_per_seq, unit(j, i) % units_per_seq, 0))] + c_out,
        out_shape=[jax.ShapeDtypeStruct((T, N), BF16), jax.ShapeDtypeStruct((Bsz, S, C), BF16)] + c_shapes,
        scratch_shapes=[pltpu.VMEM((GC_BLOCKS, D, MXU_WIDTH), BF16),
                        pltpu.VMEM((CONV_HALO + GC_ROWS, C), F32), pltpu.VMEM((GC_ROWS, C), F32)],
        compiler_params=_cparams("arbitrary", "arbitrary"),
        name="gate_conv",
    )(*args, *cast)


def _mix_out_kernel(ya_ref, yb_ref, yc_ref, g0_ref, g1_ref, g2_ref, wa_ref, wb_ref, wc_ref, wo_ref,
                    nw_ref, mod_ref, o_ref, acc_ref, *, gate_row):
    j = pl.program_id(2)
    last_j = pl.num_programs(2) - 1
    tm = o_ref.shape[0]

    def step(rb, first, last):
        for r in range(tm // rb):
            rs = slice(r * rb, (r + 1) * rb)
            m = (g0_ref[rs, :].astype(F32) * _dot(ya_ref[rs, :], wa_ref[...])
                 + g1_ref[rs, :].astype(F32) * _dot(yb_ref[rs, :], wb_ref[...])
                 + g2_ref[rs, :].astype(F32) * _dot(yc_ref[rs, :], wc_ref[...]))
            y = _dot(m.astype(BF16), wo_ref[...])
            acc = y if first else acc_ref[rs, :] + y
            if last:
                o_ref[rs, :] = (mod_ref[gate_row:gate_row + 1, :] * _rms(acc, nw_ref[...])).astype(o_ref.dtype)
            else:
                acc_ref[rs, :] = acc

    pl.when(j == 0)(functools.partial(step, tm, True, False))
    pl.when((j > 0) & (j < last_j))(functools.partial(step, tm, False, False))
    pl.when(j == last_j)(functools.partial(step, min(tm, EPI_ROWS), False, True))


def _mix_out(ya, yb, yc, gates, wa, wb, wc, wo, nw, mod, gate_row):
    Bsz, S, _ = ya.shape
    D = wo.shape[-1]
    tm = min(S, 1024)
    tn = 512
    nj = D // tn
    tok = lambda width: pl.BlockSpec((None, tm, width), lambda b, i, j: (b, i, 0))
    gspec = lambda k: pl.BlockSpec((None, tm, tn), lambda b, i, j, k=k: (b, i, k * nj + j))
    wspec = lambda K: pl.BlockSpec((K, tn), lambda b, i, j: (0, j))
    return pl.pallas_call(
        functools.partial(_mix_out_kernel, gate_row=gate_row),
        grid=(Bsz, S // tm, nj),
        in_specs=[tok(ya.shape[-1]), tok(yb.shape[-1]), tok(yc.shape[-1]),
                  gspec(0), gspec(1), gspec(2),
                  wspec(wa.shape[0]), wspec(wb.shape[0]), wspec(wc.shape[0]),
                  pl.BlockSpec((tn, D), lambda b, i, j: (j, 0)),
                  pl.BlockSpec((1, D), lambda b, i, j: (0, 0)),
                  pl.BlockSpec((None, 6, D), lambda b, i, j: (b, 0, 0))],
        out_specs=tok(D),
        out_shape=jax.ShapeDtypeStruct((Bsz, S, D), BF16),
        scratch_shapes=[pltpu.VMEM((tm, D), F32)],
        compiler_params=_cparams("arbitrary", "arbitrary", "arbitrary"),
        name="mix_out",
    )(ya, yb, yc, gates, gates, gates, wa, wb, wc, wo, nw.reshape(1, D), mod)


def _mlp_kernel(*refs, sh_row, sc_row, gate_row, emit_next):
    x_ref, d_ref, wu_ref, wd_ref, npre_ref, npost_ref, mod_ref = refs[:7]
    if emit_next:
        nnext_ref, mnext_ref, o_ref, hn_ref, xn_ref, h_ref, acc_ref = refs[7:]
    else:
        o_ref, xn_ref, h_ref, acc_ref = refs[7:]
    j = pl.program_id(2)
    last_j = pl.num_programs(2) - 1
    tm = o_ref.shape[0]

    def step(rb, first, last):
        for r in range(tm // rb):
            rs = slice(r * rb, (r + 1) * rb)
            if first:
                xn = x_ref[rs, :] + d_ref[rs, :].astype(F32)
                xn_ref[rs, :] = xn
                y = _rms(xn, npre_ref[...])
                h = (y * (1.0 + mod_ref[sc_row:sc_row + 1, :]) + mod_ref[sh_row:sh_row + 1, :]).astype(h_ref.dtype)
                h_ref[rs, :] = h
            else:
                h = h_ref[rs, :]
            u = jnp.maximum(_dot(h, wu_ref[...]), 0.0)
            y = _dot((u * u).astype(BF16), wd_ref[...])
            acc = y if first else acc_ref[rs, :] + y
            if last:
                xo = xn_ref[rs, :] + mod_ref[gate_row:gate_row + 1, :] * _rms(acc, npost_ref[...])
                o_ref[rs, :] = xo
                if emit_next:
                    hn_ref[rs, :] = (_rms(xo, nnext_ref[...]) * (1.0 + mnext_ref[1:2, :])
                                     + mnext_ref[0:1, :]).astype(hn_ref.dtype)
            else:
                acc_ref[rs, :] = acc

    pl.when(j == 0)(functools.partial(step, min(tm, EPI_ROWS), True, False))
    pl.when((j > 0) & (j < last_j))(functools.partial(step, tm, False, False))
    pl.when(j == last_j)(functools.partial(step, min(tm, EPI_ROWS), False, True))


def _mlp(x, delta, wu, wd, npre, npost, mod, next_pre=None):
    Bsz, S, D = x.shape
    FF = wu.shape[-1]
    tm = min(S, 512)
    tf = min(FF, 1024)
    tok = pl.BlockSpec((None, tm, D), lambda b, i, j: (b, i, 0))
    vspec = pl.BlockSpec((1, D), lambda b, i, j: (0, 0))
    mspec = pl.BlockSpec((None, 6, D), lambda b, i, j: (b, 0, 0))
    in_specs = [tok, tok,
                pl.BlockSpec((D, tf), lambda b, i, j: (0, j)),
                pl.BlockSpec((tf, D), lambda b, i, j: (j, 0)),
                vspec, vspec, mspec]
    args = [x, delta, wu, wd, npre.reshape(1, D), npost.reshape(1, D), mod]
    out_specs, out_shape = tok, jax.ShapeDtypeStruct((Bsz, S, D), F32)
    if next_pre is not None:
        in_specs += [vspec, mspec]
        args += [next_pre[0].reshape(1, D), next_pre[1]]
        out_specs, out_shape = [tok, tok], [out_shape, jax.ShapeDtypeStruct((Bsz, S, D), BF16)]
    return pl.pallas_call(
        functools.partial(_mlp_kernel, sh_row=3, sc_row=4, gate_row=5, emit_next=next_pre is not None),
        grid=(Bsz, S // tm, FF // tf),
        in_specs=in_specs,
        out_specs=out_specs,
        out_shape=out_shape,
        scratch_shapes=[pltpu.VMEM((tm, D), F32), pltpu.VMEM((tm, D), BF16), pltpu.VMEM((tm, D), F32)],
        compiler_params=_cparams("arbitrary", "arbitrary", "arbitrary"),
        name="mlp",
    )(*args)


def kernel(x, c, rel_bias, hgrn_lb_logits, w_ada, b_ada, mix_norm_pre, mix_norm_post, w_in, w_gate, b_gate,
           hgrn_norm_w, conv_w, conv_b, conv_ln_g, conv_ln_b, w_a_out, w_b_out, w_c_out, w_o, mlp_norm_pre,
           mlp_norm_post, w_up, w_down):
    Bsz, S, D = x.shape
    T = Bsz * S
    mod = _ada(c, w_ada, b_ada)
    lb_logits = hgrn_lb_logits.astype(F32)
    tabs = _bias_tables(rel_bias)
    h = _prenorm(x, mix_norm_pre[0], mod[0], 0, 1)
    for l in range(DEPTH):
        h = h.reshape(T, D)
        a_proj, wa, wb, wc, wo = _proj(h, w_in, l, nb=1, bw=1024, col0=0, nsteps=4,
                                       cast=(w_a_out, w_b_out, w_c_out, w_o))
        bc = _proj(h, w_in, l, nb=3, bw=256, col0=4 * A_WIDTH, nsteps=5, stacked_out=True)
        bc = bc.reshape(5, Bsz, S, C_WIDTH)
        gates, yc, wu, wd = _gate_conv(h, w_gate, b_gate, l, bc, 3, 4, conv_w[l], conv_b[l], conv_ln_g[l],
                                       conv_ln_b[l], cast=(w_up, w_down))
        ya = _hgrn(a_proj.reshape(Bsz, S, 4 * A_WIDTH), lb_logits, hgrn_norm_w[l], l)
        yb = _attn(bc, tabs)
        delta = _mix_out(ya, yb, yc, gates.reshape(Bsz, S, N_BRANCH * D), wa, wb, wc, wo,
                         mix_norm_post[l], mod[l], 2)
        if l + 1 < DEPTH:
            x, h = _mlp(x, delta, wu, wd, mlp_norm_pre[l], mlp_norm_post[l], mod[l],
                        next_pre=(mix_norm_pre[l + 1], mod[l + 1]))
        else:
            x = _mlp(x, delta, wu, wd, mlp_norm_pre[l], mlp_norm_post[l], mod[l])
    return x
```
